```python
import math
import jax, jax.numpy as jnp
from jax import lax
import numpy as np


D_MODEL = 1024
BATCH = 16
SEQ = 4096
DEPTH = 4

N_MIXERS = 2
SB_HEADS = 16
SB_HEAD_DIM = D_MODEL // SB_HEADS
SB_BLOCK = 128
HG_EXPAND = 128
HG_HEADS = D_MODEL // HG_EXPAND
HG_DK = HG_EXPAND
HG_DV = D_MODEL // HG_HEADS
HG_CHUNK = 32
D_FF = 7 * D_MODEL // 2
N_EXPERTS = 8
TOP_K = 2
N_SB = (DEPTH + 1) // 2
N_HG = DEPTH // 2
N_DENSE = (DEPTH + 1) // 2
N_MOE = DEPTH // 2
DEEPNORM_ALPHA = (2.0 * DEPTH) ** 0.25
DEEPNORM_BETA = (8.0 * DEPTH) ** -0.25
LN_EPS = 1e-5
RMS_EPS = 1e-6

kernel_name = 'hybrid_stickbreaking_hgrn2_moe_trunk'


def layer_norm(x, g, b):
    xf = x.astype(jnp.float32)
    mu = jnp.mean(xf, axis=-1, keepdims=True)
    var = jnp.mean(jnp.square(xf - mu), axis=-1, keepdims=True)
    y = (xf - mu) * lax.rsqrt(var + LN_EPS)
    return (y * g.astype(jnp.float32) + b.astype(jnp.float32)).astype(x.dtype)


def swiglu(h, w13, w2):
    a, b = jnp.split(h @ w13, 2, axis=-1)
    return (jax.nn.silu(a) * b) @ w2


def stick_breaking_attention(h, w_qkv, w_o):
    bsz, seq, _ = h.shape
    q, k, v = jnp.split(h @ w_qkv, 3, axis=-1)

    def heads(t):
        return t.reshape(bsz, seq, SB_HEADS, SB_HEAD_DIM).transpose(0, 2, 1, 3)

    q, k, v = heads(q) * (SB_HEAD_DIM ** -0.5), heads(k), heads(v)
    outs = []
    for blk in range(seq // SB_BLOCK):
        t0, t1 = blk * SB_BLOCK, (blk + 1) * SB_BLOCK
        z = jnp.einsum('bhtd,bhsd->bhts', q[:, :, t0:t1], k[:, :, :t1]).astype(jnp.float32)
        strict = jnp.arange(t1)[None, :] < jnp.arange(t0, t1)[:, None]
        log_beta = jax.nn.log_sigmoid(z)
        log_1m_beta = jnp.where(strict, log_beta - z, 0.0)
        between = lax.cumsum(log_1m_beta, axis=3, reverse=True) - log_1m_beta
        a = jnp.where(strict, jnp.exp(log_beta + between), 0.0)
        outs.append(jnp.einsum('bhts,bhsd->bhtd', a.astype(v.dtype), v[:, :, :t1]))
    o = jnp.concatenate(outs, axis=2).transpose(0, 2, 1, 3).reshape(bsz, seq, D_MODEL)
    return o @ w_o


def hgrn2_recurrence(h, w_in, norm_g, w_o, lower_bound):
    bsz, seq, _ = h.shape
    n_chunks = seq // HG_CHUNK
    q, f, i, g = jnp.split(h @ w_in, 4, axis=-1)
    lb = lower_bound.astype(jnp.float32)
    ff = f.astype(jnp.float32)
    log_forget = jnp.log(lb + (1.0 - lb) * jax.nn.sigmoid(ff))
    key_in = (1.0 - lb) * jax.nn.sigmoid(-ff)

    def chunks(t, dh):
        t = t.astype(jnp.float32).reshape(bsz, n_chunks, HG_CHUNK, HG_HEADS, dh)
        return t.transpose(1, 0, 3, 2, 4)

    xs = (chunks(q, HG_DK), chunks(key_in, HG_DK), chunks(i, HG_DV), chunks(log_forget, HG_DK))
    causal = jnp.tril(jnp.ones((HG_CHUNK, HG_CHUNK), dtype=bool))
    mid = HG_CHUNK // 2

    def chunk_step(state, inp):
        qc, kc, vc, gc = inp
        cum = jnp.cumsum(gc, axis=2)
        g_ref = cum[:, :, mid - 1:mid]
        g_end = cum[:, :, HG_CHUNK - 1:]
        scores = jnp.einsum('bhtd,bhsd->bhts', qc * jnp.exp(cum - g_ref), kc * jnp.exp(g_ref - cum))
        scores = jnp.where(causal, scores, 0.0)
        o = (jnp.einsum('bhts,bhsv->bhtv', scores, vc)
             + jnp.einsum('bhtd,bhdv->bhtv', qc * jnp.exp(cum), state))
        new_state = (jnp.exp(g_end)[:, :, 0, :, None] * state
                     + jnp.einsum('bhsd,bhsv->bhdv', kc * jnp.exp(g_end - cum), vc))
        return new_state, o

    s0 = jnp.zeros((bsz, HG_HEADS, HG_DK, HG_DV), jnp.float32)
    _, o = lax.scan(chunk_step, s0, xs)
    o = o.transpose(1, 0, 3, 2, 4).reshape(bsz, seq, HG_HEADS, HG_DV)
    o = o * lax.rsqrt(jnp.mean(jnp.square(o), axis=-1, keepdims=True) + RMS_EPS) * norm_g.astype(jnp.float32)
    gate = jax.nn.silu(g.astype(jnp.float32)).reshape(bsz, seq, HG_HEADS, HG_DV)
    return (o * gate).reshape(bsz, seq, D_MODEL).astype(h.dtype) @ w_o


def moe_swiglu(h, router, w13, w2):
    logits = (h @ router).astype(jnp.float32)
    top_logits, top_idx = lax.top_k(logits, TOP_K)
    top_w = jax.nn.softmax(top_logits, axis=-1)
    gates = jnp.sum(jax.nn.one_hot(top_idx, N_EXPERTS, dtype=jnp.float32) * top_w[..., None], axis=-2)
    gates = gates.astype(h.dtype)
    out = jnp.zeros_like(h)
    for e in range(N_EXPERTS):
        out = out + gates[..., e:e + 1] * swiglu(h, w13[e], w2[e])
    return out


def setup_inputs(seed: int = 0) -> dict:
    key = jax.random.key(seed)
    ks = jax.random.split(key, 19)

    def normal(k, shape, scale):
        return scale * jax.random.normal(k, shape, jnp.float32)

    d, f = D_MODEL, D_FF
    return {
        'x': normal(ks[0], (BATCH, SEQ, d), 1.0),
        'c': normal(ks[1], (BATCH, d), 1.0),
        'ada_w': normal(ks[2], (DEPTH, d, 6 * d), 0.5 * d ** -0.5),
        'ada_b': normal(ks[3], (DEPTH, 6 * d), 0.02),
        'ln1_g': 1.0 + normal(ks[4], (DEPTH, d), 0.02),
        'ln1_b': normal(ks[5], (DEPTH, d), 0.02),
        'ln2_g': 1.0 + normal(ks[6], (DEPTH, d), 0.02),
        'ln2_b': normal(ks[7], (DEPTH, d), 0.02),
        'sb_w_qkv': normal(ks[8], (N_SB, d, 3 * d), d ** -0.5),
        'sb_w_o': normal(ks[9], (N_SB, d, d), DEEPNORM_BETA * d ** -0.5),
        'hg_w_in': normal(ks[10], (N_HG, d, 4 * d), d ** -0.5),
        'hg_norm_g': 1.0 + normal(ks[11], (N_HG, HG_DV), 0.02),
        'hg_w_o': normal(ks[12], (N_HG, d, d), DEEPNORM_BETA * d ** -0.5),
        'hg_lb_logits': normal(ks[13], (DEPTH, d), 0.5),
        'ffn_w13': normal(ks[14], (N_DENSE, d, 2 * f), d ** -0.5),
        'ffn_w2': normal(ks[15], (N_DENSE, f, d), DEEPNORM_BETA * f ** -0.5),
        'moe_router': normal(ks[16], (N_MOE, d, N_EXPERTS), d ** -0.5),
        'moe_w13': normal(ks[17], (N_MOE, N_EXPERTS, d, 2 * f), d ** -0.5),
        'moe_w2': normal(ks[18], (N_MOE, N_EXPERTS, f, d), DEEPNORM_BETA * f ** -0.5),
    }


def reference(x, c, ada_w, ada_b, ln1_g, ln1_b, ln2_g, ln2_b, sb_w_qkv, sb_w_o, hg_w_in, hg_norm_g,
              hg_w_o, hg_lb_logits, ffn_w13, ffn_w2, moe_router, moe_w13, moe_w2):
    p = jax.nn.softmax(hg_lb_logits.astype(jnp.float32), axis=0)
    lb_table = jnp.cumsum(p, axis=0) - p[0:1]
    cond = jax.nn.silu(c)
    for layer in range(DEPTH):
        mod = (cond @ ada_w[layer] + ada_b[layer])[:, None, :]
        shift1, scale1, gate1, shift2, scale2, gate2 = jnp.split(mod, 6, axis=-1)
        j = layer // N_MIXERS
        h = x * (1.0 + scale1) + shift1
        if layer % N_MIXERS == 0:
            y = stick_breaking_attention(h, sb_w_qkv[j], sb_w_o[j])
        else:
            y = hgrn2_recurrence(h, hg_w_in[j], hg_norm_g[j], hg_w_o[j], lb_table[layer])
        x = layer_norm(DEEPNORM_ALPHA * x + gate1 * y, ln1_g[layer], ln1_b[layer])
        h = x * (1.0 + scale2) + shift2
        if layer % 2 == 0:
            y = swiglu(h, ffn_w13[j], ffn_w2[j])
        else:
            y = moe_swiglu(h, moe_router[j], moe_w13[j], moe_w2[j])
        x = layer_norm(DEEPNORM_ALPHA * x + gate2 * y, ln2_g[layer], ln2_b[layer])
    return x
```

```python
import functools

import jax
import jax.numpy as jnp
from jax import lax
from jax.experimental import pallas as pl
from jax.experimental.pallas import tpu as pltpu

F32 = jnp.float32
BF16 = jnp.bfloat16

LANES = 128
SB_HEAD_DIM = 64
HG_HEAD_DIM = 128
HG_CHUNK = 32
N_EXPERTS_TOPK = 2
LN_EPS = 1e-5
RMS_EPS = 1e-6
VMEM_LIMIT_BYTES = 56 * 1024 * 1024
SB_UNDERFLOW_BOUND = -105.0

SHIFT1, SCALE1, GATE1, SHIFT2, SCALE2, GATE2 = range(6)


def _params(sem):
    return pltpu.CompilerParams(dimension_semantics=sem, vmem_limit_bytes=VMEM_LIMIT_BYTES)


def _pick(n, candidates):
    for c in candidates:
        if n % c == 0:
            return c
    raise ValueError(f"no tile in {candidates} divides {n}")


def _sigmoid(x):
    return 1.0 / (1.0 + jnp.exp(-x))


def _dot(a, b):
    return jnp.dot(a, b, preferred_element_type=F32)


def _dot_nt(a, b):
    return lax.dot_general(a, b, (((1,), (1,)), ((), ())), preferred_element_type=F32)


def _dot_tn(a, b):
    return lax.dot_general(a, b, (((0,), (0,)), ((), ())), preferred_element_type=F32)


def _split_dot(x, m):
    hi = x.astype(BF16)
    lo = (x - hi.astype(F32)).astype(BF16)
    return _dot(hi, m) + _dot(lo, m)


def _layer_norm(r, g, b):
    mu = jnp.mean(r, axis=-1, keepdims=True)
    d = r - mu
    var = jnp.mean(d * d, axis=-1, keepdims=True)
    return d * lax.rsqrt(var + LN_EPS) * g + b


def _lb_table_kernel(g_ref, o_ref):
    g = g_ref[...]
    e = jnp.exp(g - jnp.max(g, axis=0, keepdims=True))
    p = e / jnp.sum(e, axis=0, keepdims=True)
    run = jnp.zeros_like(p[0:1])
    for l in range(g.shape[0]):
        run = run + p[l:l + 1]
        o_ref[l:l + 1, :] = run - p[0:1]


def _lb_table(logits):
    return pl.pallas_call(
        _lb_table_kernel,
        out_shape=jax.ShapeDtypeStruct(logits.shape, F32),
        name="lb_table",
    )(logits.astype(F32))


def _modulation_kernel(c_ref, w_ref, b_ref, o_ref):
    c = c_ref[...]
    cond = c * _sigmoid(c)
    y = jnp.dot(cond, w_ref[0], preferred_element_type=F32, precision=lax.Precision.HIGHEST)
    o_ref[0] = y + b_ref[0]


def _modulation(c, ada_w, ada_b):
    depth, d, n = ada_w.shape
    bsz = c.shape[0]
    tn = _pick(n, (1536, 1024, 512, 256, 128))
    return pl.pallas_call(
        _modulation_kernel,
        grid=(depth, n // tn),
        in_specs=[
            pl.BlockSpec((bsz, d), lambda l, j: (0, 0)),
            pl.BlockSpec((1, d, tn), lambda l, j: (l, 0, j)),
            pl.BlockSpec((1, 1, tn), lambda l, j: (l, 0, j)),
        ],
        out_specs=pl.BlockSpec((1, bsz, tn), lambda l, j: (l, 0, j)),
        out_shape=jax.ShapeDtypeStruct((depth, bsz, n), F32),
        compiler_params=_params(("parallel", "parallel")),
        name="modulation",
    )(c, ada_w, ada_b.reshape(depth, 1, n))


def _inproj_kernel(x_ref, m_ref, w_ref, o_ref, h_ref, *, n_chunk):
    m = m_ref[0]
    h_ref[...] = (x_ref[0] * (1.0 + m[SCALE1:SCALE1 + 1]) + m[SHIFT1:SHIFT1 + 1]).astype(BF16)
    for n in range(w_ref.shape[1] // n_chunk):
        cols = slice(n * n_chunk, (n + 1) * n_chunk)
        o_ref[0, :, cols] = _dot(h_ref[...], w_ref[:, cols]).astype(o_ref.dtype)


def _inproj(x, mod, w):
    bsz, seq, d = x.shape
    n = w.shape[1]
    tm = _pick(seq, (512, 256, 128))
    n_chunk = _pick(n, (512, 256, 128))
    return pl.pallas_call(
        functools.partial(_inproj_kernel, n_chunk=n_chunk),
        grid=(bsz, seq // tm),
        in_specs=[
            pl.BlockSpec((1, tm, d), lambda b, i: (b, i, 0)),
            pl.BlockSpec((1, 6, d), lambda b, i: (b, 0, 0)),
            pl.BlockSpec((d, n), lambda b, i: (0, 0)),
        ],
        out_specs=pl.BlockSpec((1, tm, n), lambda b, i: (b, i, 0)),
        out_shape=jax.ShapeDtypeStruct((bsz, seq, n), BF16),
        scratch_shapes=[pltpu.VMEM((tm, d), BF16)],
        compiler_params=_params(("parallel", "parallel")),
        name="inproj",
    )(x, mod, w)


def _sb_attention_kernel(q_ref, k_ref, v_ref, o_ref, carry_ref, acc_ref, *, blk):
    seq, width = q_ref.shape[1], q_ref.shape[2]
    n_pair = width // LANES
    scale = SB_HEAD_DIM ** -0.5

    lane = lax.broadcasted_iota(jnp.int32, (blk, LANES), 1)
    head_lanes = [lane < SB_HEAD_DIM, lane >= SB_HEAD_DIM]
    row = lax.broadcasted_iota(jnp.int32, (blk, blk), 0)
    col = lax.broadcasted_iota(jnp.int32, (blk, blk), 1)
    suffix = jnp.concatenate(
        [jnp.where(row > col, 1.0, 0.0), jnp.ones((blk, blk), F32)], axis=1).astype(BF16)

    def query_block(qi, _):
        q0 = pl.multiple_of(qi * blk, blk)
        carry_ref[...] = jnp.zeros_like(carry_ref)
        acc_ref[...] = jnp.zeros_like(acc_ref)

        def not_done(state):
            j, done = state
            return jnp.logical_and(j >= 0, done == 0)

        def key_block(state):
            j, _ = state
            k0 = pl.multiple_of(j * blk, blk)
            strict = (k0 + col) < (q0 + row)
            worst = jnp.full((), -jnp.inf, F32)
            for p in range(n_pair):
                lanes = slice(p * LANES, (p + 1) * LANES)
                q2 = q_ref[0, pl.ds(q0, blk), lanes]
                k2 = k_ref[0, pl.ds(k0, blk), lanes]
                v2 = v_ref[0, pl.ds(k0, blk), lanes]
                for hh in range(2):
                    qm = jnp.where(head_lanes[hh], q2, jnp.zeros_like(q2))
                    vm = jnp.where(head_lanes[hh], v2, jnp.zeros_like(v2))
                    z = _dot_nt(qm, k2) * scale
                    log_1m_beta = -(jnp.maximum(z, 0.0) + jnp.log(1.0 + jnp.exp(-jnp.abs(z))))
                    log_1m_beta = jnp.where(strict, log_1m_beta, 0.0)
                    log_beta = log_1m_beta + z
                    sums = _split_dot(log_1m_beta, suffix)
                    between = sums[:, :blk]
                    total = sums[:, blk:]
                    c_old = carry_ref[2 * p + hh]
                    a = jnp.where(strict, jnp.exp(log_beta + between + c_old), 0.0)
                    acc_ref[p] += _dot(a.astype(BF16), vm)
                    c_new = c_old + total
                    carry_ref[2 * p + hh] = c_new
                    worst = jnp.maximum(worst, jnp.max(c_new))
            done = (worst < SB_UNDERFLOW_BOUND).astype(jnp.int32)
            return j - 1, done

        lax.while_loop(not_done, key_block, (qi, jnp.int32(0)))
        for p in range(n_pair):
            o_ref[0, pl.ds(q0, blk), p * LANES:(p + 1) * LANES] = acc_ref[p].astype(o_ref.dtype)
        return 0

    lax.fori_loop(0, seq // blk, query_block, 0)


def _sb_attention(qkv, d):
    bsz, seq, _ = qkv.shape
    width = _pick(d, (256, 128))
    n_grp = d // width
    blk = 128
    spec = lambda off: pl.BlockSpec((1, seq, width), lambda b, g: (b, 0, off + g))
    return pl.pallas_call(
        functools.partial(_sb_attention_kernel, blk=blk),
        grid=(bsz, n_grp),
        in_specs=[spec(0), spec(n_grp), spec(2 * n_grp)],
        out_specs=pl.BlockSpec((1, seq, width), lambda b, g: (b, 0, g)),
        out_shape=jax.ShapeDtypeStruct((bsz, seq, d), BF16),
        scratch_shapes=[
            pltpu.VMEM((width // SB_HEAD_DIM, blk, blk), F32),
            pltpu.VMEM((width // LANES, blk, LANES), F32),
        ],
        compiler_params=_params(("parallel", "parallel")),
        name="sb_attention",
    )(qkv, qkv, qkv)


def _hgrn_kernel(q_ref, f_ref, i_ref, g_ref, lb_ref, ng_ref, o_ref, state_ref):
    ts, d = q_ref.shape[1], q_ref.shape[2]
    n_heads = d // HG_HEAD_DIM
    c = HG_CHUNK
    mid = c // 2

    @pl.when(pl.program_id(1) == 0)
    def _():
        state_ref[...] = jnp.zeros_like(state_ref)

    row = lax.broadcasted_iota(jnp.int32, (c, c), 0)
    col = lax.broadcasted_iota(jnp.int32, (c, c), 1)
    causal = col <= row
    prefix = jnp.where(causal, 1.0, 0.0).astype(BF16)
    norm_g = ng_ref[...]

    def chunk(ci, _):
        r0 = pl.multiple_of(ci * c, c)
        rows = pl.ds(r0, c)
        for h in range(n_heads):
            lanes = slice(h * HG_HEAD_DIM, (h + 1) * HG_HEAD_DIM)
            lb = lb_ref[:, lanes]
            qc = q_ref[0, rows, lanes].astype(F32)
            fc = f_ref[0, rows, lanes].astype(F32)
            vc = i_ref[0, rows, lanes]
            gc = g_ref[0, rows, lanes].astype(F32)
            e = jnp.exp(-jnp.abs(fc))
            big = 1.0 / (1.0 + e)
            small = e * big
            sig_pos = jnp.where(fc >= 0, big, small)
            sig_neg = jnp.where(fc >= 0, small, big)
            log_forget = jnp.log(lb + (1.0 - lb) * sig_pos)
            kc = (1.0 - lb) * sig_neg
            hi = log_forget.astype(BF16)
            lo = (log_forget - hi.astype(F32)).astype(BF16)
            cum = _dot(prefix, hi) + _dot(prefix, lo)
            g_ref_row = cum[mid - 1:mid]
            g_end = cum[c - 1:c]
            qd = (qc * jnp.exp(cum - g_ref_row)).astype(BF16)
            kd = (kc * jnp.exp(g_ref_row - cum)).astype(BF16)
            scores = jnp.where(causal, _dot_nt(qd, kd), 0.0)
            st = state_ref[h]
            o = _dot(scores.astype(BF16), vc) + _dot_nt((qc * jnp.exp(cum)).astype(BF16), st.astype(BF16))
            k_end = (kc * jnp.exp(g_end - cum)).astype(BF16)
            state_ref[h] = st * jnp.exp(g_end) + _dot_tn(vc, k_end)
            ms = jnp.mean(o * o, axis=-1, keepdims=True)
            y = o * lax.rsqrt(ms + RMS_EPS) * norm_g
            o_ref[0, rows, lanes] = (y * (gc * _sigmoid(gc))).astype(o_ref.dtype)
        return 0

    lax.fori_loop(0, ts // c, chunk, 0)


def _hgrn(qfig, lb_row, norm_g, d):
    bsz, seq, _ = qfig.shape
    ts = _pick(seq, (512, 256, 128))
    n_heads = d // HG_HEAD_DIM
    spec = lambda off: pl.BlockSpec((1, ts, d), lambda b, s: (b, s, off))
    return pl.pallas_call(
        _hgrn_kernel,
        grid=(bsz, seq // ts),
        in_specs=[spec(0), spec(1), spec(2), spec(3),
                  pl.BlockSpec((1, d), lambda b, s: (0, 0)),
                  pl.BlockSpec((1, HG_HEAD_DIM), lambda b, s: (0, 0))],
        out_specs=pl.BlockSpec((1, ts, d), lambda b, s: (b, s, 0)),
        out_shape=jax.ShapeDtypeStruct((bsz, seq, d), BF16),
        scratch_shapes=[pltpu.VMEM((n_heads, HG_HEAD_DIM, HG_HEAD_DIM), F32)],
        compiler_params=_params(("parallel", "arbitrary")),
        name="hgrn",
    )(qfig, qfig, qfig, qfig, lb_row.reshape(1, d), norm_g.reshape(1, HG_HEAD_DIM).astype(F32))


def _outproj_ln_kernel(y_ref, w_ref, x_ref, m_ref, g_ref, b_ref, o_ref, *, alpha):
    y = _dot(y_ref[0], w_ref[...])
    gate = m_ref[0][GATE1:GATE1 + 1]
    o_ref[0] = _layer_norm(alpha * x_ref[0] + gate * y, g_ref[...], b_ref[...])


def _outproj_ln(y, w, x, mod, ln_g, ln_b, alpha):
    bsz, seq, d = x.shape
    tm = _pick(seq, (512, 256, 128))
    tok = pl.BlockSpec((1, tm, d), lambda b, i: (b, i, 0))
    vec = pl.BlockSpec((1, d), lambda b, i: (0, 0))
    return pl.pallas_call(
        functools.partial(_outproj_ln_kernel, alpha=alpha),
        grid=(bsz, seq // tm),
        in_specs=[tok, pl.BlockSpec((d, d), lambda b, i: (0, 0)), tok,
                  pl.BlockSpec((1, 6, d), lambda b, i: (b, 0, 0)), vec, vec],
        out_specs=tok,
        out_shape=jax.ShapeDtypeStruct((bsz, seq, d), F32),
        compiler_params=_params(("parallel", "parallel")),
        name="outproj_ln",
    )(y, w, x, mod, ln_g.reshape(1, d), ln_b.reshape(1, d))


def _router_kernel(x_ref, m_ref, r_ref, o_ref, *, n_experts):
    m = m_ref[0]
    h = (x_ref[0] * (1.0 + m[SCALE2:SCALE2 + 1]) + m[SHIFT2:SHIFT2 + 1]).astype(BF16)
    logits = _dot(h, r_ref[...])
    lane = lax.broadcasted_iota(jnp.int32, logits.shape, 1)
    logits = jnp.where(lane < n_experts, logits, -jnp.inf)
    m1 = jnp.max(logits, axis=-1, keepdims=True)
    i1 = jnp.min(jnp.where(logits == m1, lane, LANES), axis=-1, keepdims=True)
    rest = jnp.where(lane == i1, -jnp.inf, logits)
    m2 = jnp.max(rest, axis=-1, keepdims=True)
    i2 = jnp.min(jnp.where(rest == m2, lane, LANES), axis=-1, keepdims=True)
    t = jnp.exp(m2 - m1)
    w1 = 1.0 / (1.0 + t)
    o_ref[0] = jnp.where(lane == i1, w1, 0.0) + jnp.where(lane == i2, t * w1, 0.0)


def _router(x, mod, router_w):
    bsz, seq, d = x.shape
    n_experts = router_w.shape[1]
    tm = _pick(seq, (512, 256, 128))
    r = jnp.zeros((d, LANES), BF16).at[:, :n_experts].set(router_w.astype(BF16))
    return pl.pallas_call(
        functools.partial(_router_kernel, n_experts=n_experts),
        grid=(bsz, seq // tm),
        in_specs=[pl.BlockSpec((1, tm, d), lambda b, i: (b, i, 0)),
                  pl.BlockSpec((1, 6, d), lambda b, i: (b, 0, 0)),
                  pl.BlockSpec((d, LANES), lambda b, i: (0, 0))],
        out_specs=pl.BlockSpec((1, tm, LANES), lambda b, i: (b, i, 0)),
        out_shape=jax.ShapeDtypeStruct((bsz, seq, LANES), F32),
        compiler_params=_params(("parallel", "parallel")),
        name="router",
    )(x, mod, r)


def _ffn_kernel(*refs, alpha, gated):
    if gated:
        x_ref, m_ref, wa_ref, wb_ref, w2_ref, g_ref, b_ref, gates_ref, o_ref, h_ref, acc_ref = refs
    else:
        x_ref, m_ref, wa_ref, wb_ref, w2_ref, g_ref, b_ref, o_ref, h_ref, acc_ref = refs
    e, f = pl.program_id(2), pl.program_id(3)
    m = m_ref[0]

    @pl.when(jnp.logical_and(e == 0, f == 0))
    def _():
        h_ref[...] = (x_ref[0] * (1.0 + m[SCALE2:SCALE2 + 1]) + m[SHIFT2:SHIFT2 + 1]).astype(BF16)
        acc_ref[...] = jnp.zeros_like(acc_ref)

    h = h_ref[...]
    a = _dot(h, wa_ref[0])
    b = _dot(h, wb_ref[0])
    u = a * _sigmoid(a) * b
    if gated:
        gates = gates_ref[0]
        lane = lax.broadcasted_iota(jnp.int32, gates.shape, 1)
        u = u * jnp.sum(jnp.where(lane == e, gates, 0.0), axis=-1, keepdims=True)
    acc_ref[...] += _dot(u.astype(BF16), w2_ref[0])

    @pl.when(jnp.logical_and(e == pl.num_programs(2) - 1, f == pl.num_programs(3) - 1))
    def _():
        r = alpha * x_ref[0] + m[GATE2:GATE2 + 1] * acc_ref[...]
        o_ref[0] = _layer_norm(r, g_ref[...], b_ref[...])


def _ffn(x, mod, w13, w2, ln_g, ln_b, alpha, gates=None):
    bsz, seq, d = x.shape
    n_exp, ff = w2.shape[0], w2.shape[1]
    tm = _pick(seq, (1024, 512, 256, 128))
    tf = _pick(ff, (512, 256, 128))
    nf = ff // tf
    tok = pl.BlockSpec((1, tm, d), lambda b, i, e, f: (b, i, 0))
    vec = pl.BlockSpec((1, d), lambda b, i, e, f: (0, 0))
    in_specs = [tok, pl.BlockSpec((1, 6, d), lambda b, i, e, f: (b, 0, 0)),
                pl.BlockSpec((1, d, tf), lambda b, i, e, f: (e, 0, f)),
                pl.BlockSpec((1, d, tf), lambda b, i, e, f: (e, 0, nf + f)),
                pl.BlockSpec((1, tf, d), lambda b, i, e, f: (e, f, 0)),
                vec, vec]
    args = [x, mod, w13, w13, w2, ln_g.reshape(1, d), ln_b.reshape(1, d)]
    if gates is not None:
        in_specs.append(pl.BlockSpec((1, tm, LANES), lambda b, i, e, f: (b, i, 0)))
        args.append(gates)
    return pl.pallas_call(
        functools.partial(_ffn_kernel, alpha=alpha, gated=gates is not None),
        grid=(bsz, seq // tm, n_exp, nf),
        in_specs=in_specs,
        out_specs=tok,
        out_shape=jax.ShapeDtypeStruct((bsz, seq, d), F32),
        scratch_shapes=[pltpu.VMEM((tm, d), BF16), pltpu.VMEM((tm, d), F32)],
        compiler_params=_params(("parallel", "parallel", "arbitrary", "arbitrary")),
        name="moe_ffn" if gates is not None else "ffn",
    )(*args)


def kernel(x, c, ada_w, ada_b, ln1_g, ln1_b, ln2_g, ln2_b, sb_w_qkv, sb_w_o, hg_w_in, hg_norm_g,
           hg_w_o, hg_lb_logits, ffn_w13, ffn_w2, moe_router, moe_w13, moe_w2):
    bsz, seq, d = x.shape
    depth = ada_w.shape[0]
    alpha = (2.0 * depth) ** 0.25

    lb_table = _lb_table(hg_lb_logits)
    mod_all = _modulation(c, ada_w, ada_b).reshape(depth, bsz, 6, d)

    for layer in range(depth):
        mod = mod_all[layer]
        j = layer // 2
        if layer % 2 == 0:
            qkv = _inproj(x, mod, sb_w_qkv[j].astype(BF16))
            y = _sb_attention(qkv, d)
            x = _outproj_ln(y, sb_w_o[j].astype(BF16), x, mod, ln1_g[layer], ln1_b[layer], alpha)
            x = _ffn(x, mod, ffn_w13[j][None].astype(BF16), ffn_w2[j][None].astype(BF16),
                     ln2_g[layer], ln2_b[layer], alpha)
        else:
            qfig = _inproj(x, mod, hg_w_in[j].astype(BF16))
            y = _hgrn(qfig, lb_table[layer], hg_norm_g[j], d)
            x = _outproj_ln(y, hg_w_o[j].astype(BF16), x, mod, ln1_g[layer], ln1_b[layer], alpha)
            gates = _router(x, mod, moe_router[j])
            x = _ffn(x, mod, moe_w13[j].astype(BF16), moe_w2[j].astype(BF16),
                     ln2_g[layer], ln2_b[layer], alpha, gates=gates)
    return x
```

```python
import functools

import jax
import jax.numpy as jnp
from jax import lax
from jax.experimental import pallas as pl
from jax.experimental.pallas import tpu as pltpu

F32 = jnp.float32
BF16 = jnp.bfloat16

LANES = 128
SB_HEAD_DIM = 64
HG_HEAD_DIM = 128
HG_CHUNK = 32
N_EXPERTS_TOPK = 2
LN_EPS = 1e-5
RMS_EPS = 1e-6
VMEM_LIMIT_BYTES = 56 * 1024 * 1024
SB_UNDERFLOW_BOUND = -105.0

SHIFT1, SCALE1, GATE1, SHIFT2, SCALE2, GATE2 = range(6)


def _params(sem):
    return pltpu.CompilerParams(dimension_semantics=sem, vmem_limit_bytes=VMEM_LIMIT_BYTES)


def _pick(n, candidates):
    for c in candidates:
        if n % c == 0:
            return c
    raise ValueError(f"no tile in {candidates} divides {n}")


def _sigmoid(x):
    return 1.0 / (1.0 + jnp.exp(-x))


def _dot(a, b):
    return jnp.dot(a, b, preferred_element_type=F32)


def _dot_nt(a, b):
    return lax.dot_general(a, b, (((1,), (1,)), ((), ())), preferred_element_type=F32)


def _dot_tn(a, b):
    return lax.dot_general(a, b, (((0,), (0,)), ((), ())), preferred_element_type=F32)


def _split_dot(x, m):
    hi = x.astype(BF16)
    lo = (x - hi.astype(F32)).astype(BF16)
    return _dot(hi, m) + _dot(lo, m)


def _layer_norm(r, g, b):
    mu = jnp.mean(r, axis=-1, keepdims=True)
    d = r - mu
    var = jnp.mean(d * d, axis=-1, keepdims=True)
    return d * lax.rsqrt(var + LN_EPS) * g + b


def _lb_table_kernel(g_ref, o_ref):
    g = g_ref[...]
    e = jnp.exp(g - jnp.max(g, axis=0, keepdims=True))
    p = e / jnp.sum(e, axis=0, keepdims=True)
    run = jnp.zeros_like(p[0:1])
    for l in range(g.shape[0]):
        run = run + p[l:l + 1]
        o_ref[l:l + 1, :] = run - p[0:1]


def _lb_table(logits):
    return pl.pallas_call(
        _lb_table_kernel,
        out_shape=jax.ShapeDtypeStruct(logits.shape, F32),
        name="lb_table",
    )(logits.astype(F32))


def _modulation_kernel(c_ref, w_ref, b_ref, o_ref):
    c = c_ref[...]
    cond = c * _sigmoid(c)
    y = jnp.dot(cond, w_ref[0], preferred_element_type=F32, precision=lax.Precision.HIGHEST)
    o_ref[0] = y + b_ref[0]


def _modulation(c, ada_w, ada_b):
    depth, d, n = ada_w.shape
    bsz = c.shape[0]
    tn = _pick(n, (1536, 1024, 512, 256, 128))
    return pl.pallas_call(
        _modulation_kernel,
        grid=(depth, n // tn),
        in_specs=[
            pl.BlockSpec((bsz, d), lambda l, j: (0, 0)),
            pl.BlockSpec((1, d, tn), lambda l, j: (l, 0, j)),
            pl.BlockSpec((1, 1, tn), lambda l, j: (l, 0, j)),
        ],
        out_specs=pl.BlockSpec((1, bsz, tn), lambda l, j: (l, 0, j)),
        out_shape=jax.ShapeDtypeStruct((depth, bsz, n), F32),
        compiler_params=_params(("parallel", "parallel")),
        name="modulation",
    )(c, ada_w, ada_b.reshape(depth, 1, n))


def _inproj_kernel(x_ref, m_ref, w_ref, o_ref, h_ref, *, n_chunk):
    m = m_ref[0]
    h_ref[...] = (x_ref[0] * (1.0 + m[SCALE1:SCALE1 + 1]) + m[SHIFT1:SHIFT1 + 1]).astype(BF16)
    for n in range(w_ref.shape[1] // n_chunk):
        cols = slice(n * n_chunk, (n + 1) * n_chunk)
        o_ref[0, :, cols] = _dot(h_ref[...], w_ref[:, cols]).astype(o_ref.dtype)


def _inproj(x, mod, w):
    bsz, seq, d = x.shape
    n = w.shape[1]
    tm = _pick(seq, (512, 256, 128))
    n_chunk = _pick(n, (512, 256, 128))
    return pl.pallas_call(
        functools.partial(_inproj_kernel, n_chunk=n_chunk),
        grid=(bsz, seq // tm),
        in_specs=[
            pl.BlockSpec((1, tm, d), lambda b, i: (b, i, 0)),
            pl.BlockSpec((1, 6, d), lambda b, i: (b, 0, 0)),
            pl.BlockSpec((d, n), lambda b, i: (0, 0)),
        ],
        out_specs=pl.BlockSpec((1, tm, n), lambda b, i: (b, i, 0)),
        out_shape=jax.ShapeDtypeStruct((bsz, seq, n), BF16),
        scratch_shapes=[pltpu.VMEM((tm, d), BF16)],
        compiler_params=_params(("parallel", "parallel")),
        name="inproj",
    )(x, mod, w)


def _sb_attention_kernel(q_ref, k_ref, v_ref, o_ref, qs_ref, carry_ref, acc_ref, *, blk):
    seq, width = q_ref.shape[1], q_ref.shape[2]
    n_pair = width // LANES
    n_head = 2 * n_pair
    scale = SB_HEAD_DIM ** -0.5

    lane = lax.broadcasted_iota(jnp.int32, (blk, LANES), 1)
    head_lanes = [lane < SB_HEAD_DIM, lane >= SB_HEAD_DIM]
    row = lax.broadcasted_iota(jnp.int32, (blk, blk), 0)
    col = lax.broadcasted_iota(jnp.int32, (blk, blk), 1)
    suffix = jnp.concatenate(
        [jnp.where(row > col, 1.0, 0.0), jnp.ones((blk, blk), F32)], axis=1).astype(BF16)
    srow = lax.broadcasted_iota(jnp.int32, (n_head * blk, blk), 0) & (blk - 1)
    scol = lax.broadcasted_iota(jnp.int32, (n_head * blk, blk), 1)
    diag_strict = scol < srow

    def key_block(k0, on_diagonal):
        z = jnp.concatenate(
            [_dot_nt(qs_ref[p * 2 * blk:(p + 1) * 2 * blk, :],
                     k_ref[0, pl.ds(k0, blk), p * LANES:(p + 1) * LANES]) for p in range(n_pair)],
            axis=0) * scale
        log_1m_beta = -(jnp.maximum(z, 0.0) + jnp.log(1.0 + jnp.exp(-jnp.abs(z))))
        if on_diagonal:
            log_1m_beta = jnp.where(diag_strict, log_1m_beta, 0.0)
        log_beta = log_1m_beta + z
        sums = _split_dot(log_1m_beta, suffix)
        c_old = carry_ref[...]
        a = jnp.exp(log_beta + sums[:, :blk] + c_old)
        if on_diagonal:
            a = jnp.where(diag_strict, a, 0.0)
        a = a.astype(BF16)
        carry_ref[...] = c_old + sums[:, blk:]
        for p in range(n_pair):
            v2 = v_ref[0, pl.ds(k0, blk), p * LANES:(p + 1) * LANES]
            upd = None
            for hh in range(2):
                h = 2 * p + hh
                vm = jnp.where(head_lanes[hh], v2, jnp.zeros_like(v2))
                t = _dot(a[h * blk:(h + 1) * blk], vm)
                upd = t if upd is None else upd + t
            acc_ref[p] += upd

    def all_underflowed():
        return (jnp.max(carry_ref[...]) < SB_UNDERFLOW_BOUND).astype(jnp.int32)

    n_peeled = 2

    def query_block(qi, _):
        q0 = pl.multiple_of(qi * blk, blk)
        carry_ref[...] = jnp.zeros_like(carry_ref)
        acc_ref[...] = jnp.zeros_like(acc_ref)
        for p in range(n_pair):
            q2 = q_ref[0, pl.ds(q0, blk), p * LANES:(p + 1) * LANES]
            for hh in range(2):
                h = 2 * p + hh
                qs_ref[h * blk:(h + 1) * blk, :] = jnp.where(head_lanes[hh], q2, jnp.zeros_like(q2))

        key_block(q0, True)
        for n in range(1, n_peeled + 1):
            @pl.when(qi >= n)
            def _():
                key_block(pl.multiple_of((qi - n) * blk, blk), False)

        def not_done(state):
            j, done = state
            return jnp.logical_and(j >= 0, done == 0)

        def sweep(state):
            j, _ = state
            key_block(pl.multiple_of(j * blk, blk), False)
            return j - 1, all_underflowed()

        lax.while_loop(not_done, sweep, (qi - (n_peeled + 1), all_underflowed()))
        for p in range(n_pair):
            o_ref[0, pl.ds(q0, blk), p * LANES:(p + 1) * LANES] = acc_ref[p].astype(o_ref.dtype)
        return 0

    lax.fori_loop(0, seq // blk, query_block, 0)


def _sb_attention(qkv, d):
    bsz, seq, _ = qkv.shape
    width = _pick(d, (256, 128))
    n_grp = d // width
    blk = 128
    spec = lambda off: pl.BlockSpec((1, seq, width), lambda b, g: (b, 0, off + g))
    return pl.pallas_call(
        functools.partial(_sb_attention_kernel, blk=blk),
        grid=(bsz, n_grp),
        in_specs=[spec(0), spec(n_grp), spec(2 * n_grp)],
        out_specs=pl.BlockSpec((1, seq, width), lambda b, g: (b, 0, g)),
        out_shape=jax.ShapeDtypeStruct((bsz, seq, d), BF16),
        scratch_shapes=[
            pltpu.VMEM((width // SB_HEAD_DIM * blk, LANES), BF16),
            pltpu.VMEM((width // SB_HEAD_DIM * blk, blk), F32),
            pltpu.VMEM((width // LANES, blk, LANES), F32),
        ],
        compiler_params=_params(("parallel", "parallel")),
        name="sb_attention",
    )(qkv, qkv, qkv)


def _hgrn_kernel(q_ref, f_ref, i_ref, g_ref, lb_ref, ng_ref, o_ref, state_ref):
    ts, d = q_ref.shape[1], q_ref.shape[2]
    n_heads = d // HG_HEAD_DIM
    c = HG_CHUNK
    mid = c // 2

    @pl.when(pl.program_id(1) == 0)
    def _():
        state_ref[...] = jnp.zeros_like(state_ref)

    row = lax.broadcasted_iota(jnp.int32, (c, c), 0)
    col = lax.broadcasted_iota(jnp.int32, (c, c), 1)
    causal = col <= row
    prefix = jnp.where(causal, 1.0, 0.0).astype(BF16)
    lb = lb_ref[...]
    one_m_lb = 1.0 - lb
    norm_g = jnp.concatenate([ng_ref[...]] * n_heads, axis=1)

    def chunk(ci, states):
        rows = pl.ds(pl.multiple_of(ci * c, c), c)
        qc = q_ref[0, rows, :].astype(F32)
        fc = f_ref[0, rows, :].astype(F32)
        vc = i_ref[0, rows, :]
        gc = g_ref[0, rows, :].astype(F32)
        e = jnp.exp(-jnp.abs(fc))
        big = 1.0 / (1.0 + e)
        small = e * big
        sig_pos = jnp.where(fc >= 0, big, small)
        sig_neg = jnp.where(fc >= 0, small, big)
        log_forget = jnp.log(lb + one_m_lb * sig_pos)
        kc = one_m_lb * sig_neg
        hi = log_forget.astype(BF16)
        lo = (log_forget - hi.astype(F32)).astype(BF16)
        cum = _dot(prefix, hi) + _dot(prefix, lo)
        g_mid = cum[mid - 1:mid]
        g_end = cum[c - 1:c]
        qd = (qc * jnp.exp(cum - g_mid)).astype(BF16)
        kd = (kc * jnp.exp(g_mid - cum)).astype(BF16)
        q_in = (qc * jnp.exp(cum)).astype(BF16)
        k_end = (kc * jnp.exp(g_end - cum)).astype(BF16)
        decay = jnp.exp(g_end)
        outs, new_states = [], []
        for h in range(n_heads):
            lanes = slice(h * HG_HEAD_DIM, (h + 1) * HG_HEAD_DIM)
            st = states[h]
            scores = jnp.where(causal, _dot_nt(qd[:, lanes], kd[:, lanes]), 0.0)
            o = _dot(scores.astype(BF16), vc[:, lanes]) + _dot_nt(q_in[:, lanes], st.astype(BF16))
            new_states.append(st * decay[:, lanes] + _dot_tn(vc[:, lanes], k_end[:, lanes]))
            ms = jnp.mean(o * o, axis=-1, keepdims=True)
            outs.append(o * lax.rsqrt(ms + RMS_EPS))
        y = jnp.concatenate(outs, axis=1) * norm_g * (gc * _sigmoid(gc))
        o_ref[0, rows, :] = y.astype(o_ref.dtype)
        return tuple(new_states)

    states = lax.fori_loop(0, ts // c, chunk, tuple(state_ref[h] for h in range(n_heads)))
    for h in range(n_heads):
        state_ref[h] = states[h]


def _hgrn(qfig, lb_row, norm_g, d):
    bsz, seq, _ = qfig.shape
    ts = _pick(seq, (512, 256, 128))
    n_heads = d // HG_HEAD_DIM
    spec = lambda off: pl.BlockSpec((1, ts, d), lambda b, s: (b, s, off))
    return pl.pallas_call(
        _hgrn_kernel,
        grid=(bsz, seq // ts),
        in_specs=[spec(0), spec(1), spec(2), spec(3),
                  pl.BlockSpec((1, d), lambda b, s: (0, 0)),
                  pl.BlockSpec((1, HG_HEAD_DIM), lambda b, s: (0, 0))],
        out_specs=pl.BlockSpec((1, ts, d), lambda b, s: (b, s, 0)),
        out_shape=jax.ShapeDtypeStruct((bsz, seq, d), BF16),
        scratch_shapes=[pltpu.VMEM((n_heads, HG_HEAD_DIM, HG_HEAD_DIM), F32)],
        compiler_params=_params(("parallel", "arbitrary")),
        name="hgrn",
    )(qfig, qfig, qfig, qfig, lb_row.reshape(1, d), norm_g.reshape(1, HG_HEAD_DIM).astype(F32))


def _outproj_ln_kernel(y_ref, w_ref, x_ref, m_ref, g_ref, b_ref, o_ref, *, alpha):
    y = _dot(y_ref[0], w_ref[...])
    gate = m_ref[0][GATE1:GATE1 + 1]
    o_ref[0] = _layer_norm(alpha * x_ref[0] + gate * y, g_ref[...], b_ref[...])


def _outproj_ln(y, w, x, mod, ln_g, ln_b, alpha):
    bsz, seq, d = x.shape
    tm = _pick(seq, (512, 256, 128))
    tok = pl.BlockSpec((1, tm, d), lambda b, i: (b, i, 0))
    vec = pl.BlockSpec((1, d), lambda b, i: (0, 0))
    return pl.pallas_call(
        functools.partial(_outproj_ln_kernel, alpha=alpha),
        grid=(bsz, seq // tm),
        in_specs=[tok, pl.BlockSpec((d, d), lambda b, i: (0, 0)), tok,
                  pl.BlockSpec((1, 6, d), lambda b, i: (b, 0, 0)), vec, vec],
        out_specs=tok,
        out_shape=jax.ShapeDtypeStruct((bsz, seq, d), F32),
        compiler_params=_params(("parallel", "parallel")),
        name="outproj_ln",
    )(y, w, x, mod, ln_g.reshape(1, d), ln_b.reshape(1, d))


def _router_kernel(x_ref, m_ref, r_ref, o_ref, *, n_experts):
    m = m_ref[0]
    h = (x_ref[0] * (1.0 + m[SCALE2:SCALE2 + 1]) + m[SHIFT2:SHIFT2 + 1]).astype(BF16)
    logits = _dot(h, r_ref[...])
    lane = lax.broadcasted_iota(jnp.int32, logits.shape, 1)
    logits = jnp.where(lane < n_experts, logits, -jnp.inf)
    m1 = jnp.max(logits, axis=-1, keepdims=True)
    i1 = jnp.min(jnp.where(logits == m1, lane, LANES), axis=-1, keepdims=True)
    rest = jnp.where(lane == i1, -jnp.inf, logits)
    m2 = jnp.max(rest, axis=-1, keepdims=True)
    i2 = jnp.min(jnp.where(rest == m2, lane, LANES), axis=-1, keepdims=True)
    t = jnp.exp(m2 - m1)
    w1 = 1.0 / (1.0 + t)
    o_ref[0] = jnp.where(lane == i1, w1, 0.0) + jnp.where(lane == i2, t * w1, 0.0)


def _router(x, mod, router_w):
    bsz, seq, d = x.shape
    n_experts = router_w.shape[1]
    tm = _pick(seq, (512, 256, 128))
    r = jnp.zeros((d, LANES), BF16).at[:, :n_experts].set(router_w.astype(BF16))
    return pl.pallas_call(
        functools.partial(_router_kernel, n_experts=n_experts),
        grid=(bsz, seq // tm),
        in_specs=[pl.BlockSpec((1, tm, d), lambda b, i: (b, i, 0)),
                  pl.BlockSpec((1, 6, d), lambda b, i: (b, 0, 0)),
                  pl.BlockSpec((d, LANES), lambda b, i: (0, 0))],
        out_specs=pl.BlockSpec((1, tm, LANES), lambda b, i: (b, i, 0)),
        out_shape=jax.ShapeDtypeStruct((bsz, seq, LANES), F32),
        compiler_params=_params(("parallel", "parallel")),
        name="router",
    )(x, mod, r)


def _ffn_kernel(*refs, alpha, gated):
    if gated:
        x_ref, m_ref, wa_ref, wb_ref, w2_ref, g_ref, b_ref, gates_ref, o_ref, h_ref, acc_ref = refs
    else:
        x_ref, m_ref, wa_ref, wb_ref, w2_ref, g_ref, b_ref, o_ref, h_ref, acc_ref = refs
    e, f = pl.program_id(2), pl.program_id(3)
    m = m_ref[0]

    @pl.when(jnp.logical_and(e == 0, f == 0))
    def _():
        h_ref[...] = (x_ref[0] * (1.0 + m[SCALE2:SCALE2 + 1]) + m[SHIFT2:SHIFT2 + 1]).astype(BF16)
        acc_ref[...] = jnp.zeros_like(acc_ref)

    h = h_ref[...]
    a = _dot(h, wa_ref[0])
    b = _dot(h, wb_ref[0])
    u = a * _sigmoid(a) * b
    if gated:
        gates = gates_ref[0]
        lane = lax.broadcasted_iota(jnp.int32, gates.shape, 1)
        u = u * jnp.sum(jnp.where(lane == e, gates, 0.0), axis=-1, keepdims=True)
    acc_ref[...] += _dot(u.astype(BF16), w2_ref[0])

    @pl.when(jnp.logical_and(e == pl.num_programs(2) - 1, f == pl.num_programs(3) - 1))
    def _():
        r = alpha * x_ref[0] + m[GATE2:GATE2 + 1] * acc_ref[...]
        o_ref[0] = _layer_norm(r, g_ref[...], b_ref[...])


def _ffn(x, mod, w13, w2, ln_g, ln_b, alpha, gates=None):
    bsz, seq, d = x.shape
    n_exp, ff = w2.shape[0], w2.shape[1]
    tm = _pick(seq, (1024, 512, 256, 128))
    tf = _pick(ff, (512, 256, 128))
    nf = ff // tf
    tok = pl.BlockSpec((1, tm, d), lambda b, i, e, f: (b, i, 0))
    vec = pl.BlockSpec((1, d), lambda b, i, e, f: (0, 0))
    in_specs = [tok, pl.BlockSpec((1, 6, d), lambda b, i, e, f: (b, 0, 0)),
                pl.BlockSpec((1, d, tf), lambda b, i, e, f: (e, 0, f)),
                pl.BlockSpec((1, d, tf), lambda b, i, e, f: (e, 0, nf + f)),
                pl.BlockSpec((1, tf, d), lambda b, i, e, f: (e, f, 0)),
                vec, vec]
    args = [x, mod, w13, w13, w2, ln_g.reshape(1, d), ln_b.reshape(1, d)]
    if gates is not None:
        in_specs.append(pl.BlockSpec((1, tm, LANES), lambda b, i, e, f: (b, i, 0)))
        args.append(gates)
    return pl.pallas_call(
        functools.partial(_ffn_kernel, alpha=alpha, gated=gates is not None),
        grid=(bsz, seq // tm, n_exp, nf),
        in_specs=in_specs,
        out_specs=tok,
        out_shape=jax.ShapeDtypeStruct((bsz, seq, d), F32),
        scratch_shapes=[pltpu.VMEM((tm, d), BF16), pltpu.VMEM((tm, d), F32)],
        compiler_params=_params(("parallel", "parallel", "arbitrary", "arbitrary")),
        name="moe_ffn" if gates is not None else "ffn",
    )(*args)


def kernel(x, c, ada_w, ada_b, ln1_g, ln1_b, ln2_g, ln2_b, sb_w_qkv, sb_w_o, hg_w_in, hg_norm_g,
           hg_w_o, hg_lb_logits, ffn_w13, ffn_w2, moe_router, moe_w13, moe_w2):
    bsz, seq, d = x.shape
    depth = ada_w.shape[0]
    alpha = (2.0 * depth) ** 0.25

    lb_table = _lb_table(hg_lb_logits)
    mod_all = _modulation(c, ada_w, ada_b).reshape(depth, bsz, 6, d)

    for layer in range(depth):
        mod = mod_all[layer]
        j = layer // 2
        if layer % 2 == 0:
            qkv = _inproj(x, mod, sb_w_qkv[j].astype(BF16))
            y = _sb_attention(qkv, d)
            x = _outproj_ln(y, sb_w_o[j].astype(BF16), x, mod, ln1_g[layer], ln1_b[layer], alpha)
            x = _ffn(x, mod, ffn_w13[j][None].astype(BF16), ffn_w2[j][None].astype(BF16),
                     ln2_g[layer], ln2_b[layer], alpha)
        else:
            qfig = _inproj(x, mod, hg_w_in[j].astype(BF16))
            y = _hgrn(qfig, lb_table[layer], hg_norm_g[j], d)
            x = _outproj_ln(y, hg_w_o[j].astype(BF16), x, mod, ln1_g[layer], ln1_b[layer], alpha)
            gates = _router(x, mod, moe_router[j])
            x = _ffn(x, mod, moe_w13[j].astype(BF16), moe_w2[j].astype(BF16),
                     ln2_g[layer], ln2_b[layer], alpha, gates=gates)
    return x
```

```python
import functools

import jax
import jax.numpy as jnp
from jax import lax
from jax.experimental import pallas as pl
from jax.experimental.pallas import tpu as pltpu

F32 = jnp.float32
BF16 = jnp.bfloat16

LANES = 128
SB_HEAD_DIM = 64
HG_HEAD_DIM = 128
HG_CHUNK = 32
LN_EPS = 1e-5
RMS_EPS = 1e-6
VMEM_LIMIT_BYTES = 56 * 1024 * 1024
SB_UNDERFLOW_BOUND = -105.0

SHIFT1, SCALE1, GATE1, SHIFT2, SCALE2, GATE2 = range(6)


def _params(sem):
    return pltpu.CompilerParams(dimension_semantics=sem, vmem_limit_bytes=VMEM_LIMIT_BYTES)


def _pick(n, candidates):
    for c in candidates:
        if n % c == 0:
            return c
    raise ValueError(f"no tile in {candidates} divides {n}")


def _sigmoid(x):
    return 1.0 / (1.0 + jnp.exp(-x))


def _dot(a, b):
    return jnp.dot(a, b, preferred_element_type=F32)


def _dot_nt(a, b):
    return lax.dot_general(a, b, (((1,), (1,)), ((), ())), preferred_element_type=F32)


def _dot_tn(a, b):
    return lax.dot_general(a, b, (((0,), (0,)), ((), ())), preferred_element_type=F32)


def _split_dot(x, m):
    hi = x.astype(BF16)
    lo = (x - hi.astype(F32)).astype(BF16)
    return _dot(hi, m) + _dot(lo, m)


def _layer_norm(r, g, b):
    mu = jnp.mean(r, axis=-1, keepdims=True)
    d = r - mu
    var = jnp.mean(d * d, axis=-1, keepdims=True)
    return d * lax.rsqrt(var + LN_EPS) * g + b


def _lb_table_kernel(g_ref, o_ref):
    g = g_ref[...]
    e = jnp.exp(g - jnp.max(g, axis=0, keepdims=True))
    p = e / jnp.sum(e, axis=0, keepdims=True)
    run = jnp.zeros_like(p[0:1])
    for l in range(g.shape[0]):
        run = run + p[l:l + 1]
        o_ref[l:l + 1, :] = run - p[0:1]


def _lb_table(logits):
    return pl.pallas_call(
        _lb_table_kernel,
        out_shape=jax.ShapeDtypeStruct(logits.shape, F32),
        name="lb_table",
    )(logits.astype(F32))


def _modulation_kernel(c_ref, w_ref, b_ref, o_ref):
    c = c_ref[...]
    cond = c * _sigmoid(c)
    y = jnp.dot(cond, w_ref[0], preferred_element_type=F32, precision=lax.Precision.HIGHEST)
    o_ref[0] = y + b_ref[0]


def _modulation(c, ada_w, ada_b):
    depth, d, n = ada_w.shape
    bsz = c.shape[0]
    tn = _pick(n, (1536, 1024, 512, 256, 128))
    return pl.pallas_call(
        _modulation_kernel,
        grid=(depth, n // tn),
        in_specs=[
            pl.BlockSpec((bsz, d), lambda l, j: (0, 0)),
            pl.BlockSpec((1, d, tn), lambda l, j: (l, 0, j)),
            pl.BlockSpec((1, 1, tn), lambda l, j: (l, 0, j)),
        ],
        out_specs=pl.BlockSpec((1, bsz, tn), lambda l, j: (l, 0, j)),
        out_shape=jax.ShapeDtypeStruct((depth, bsz, n), F32),
        compiler_params=_params(("parallel", "parallel")),
        name="modulation",
    )(c, ada_w, ada_b.reshape(depth, 1, n))


def _inproj_kernel(x_ref, m_ref, w_ref, o_ref, h_ref, *, n_chunk):
    m = m_ref[0]
    h_ref[...] = (x_ref[0] * (1.0 + m[SCALE1:SCALE1 + 1]) + m[SHIFT1:SHIFT1 + 1]).astype(BF16)
    for n in range(w_ref.shape[1] // n_chunk):
        cols = slice(n * n_chunk, (n + 1) * n_chunk)
        o_ref[0, :, cols] = _dot(h_ref[...], w_ref[:, cols]).astype(o_ref.dtype)


def _inproj(x, mod, w):
    bsz, seq, d = x.shape
    n = w.shape[1]
    tm = _pick(seq, (512, 256, 128))
    n_chunk = _pick(n, (512, 256, 128))
    return pl.pallas_call(
        functools.partial(_inproj_kernel, n_chunk=n_chunk),
        grid=(bsz, seq // tm),
        in_specs=[
            pl.BlockSpec((1, tm, d), lambda b, i: (b, i, 0)),
            pl.BlockSpec((1, 6, d), lambda b, i: (b, 0, 0)),
            pl.BlockSpec((d, n), lambda b, i: (0, 0)),
        ],
        out_specs=pl.BlockSpec((1, tm, n), lambda b, i: (b, i, 0)),
        out_shape=jax.ShapeDtypeStruct((bsz, seq, n), BF16),
        scratch_shapes=[pltpu.VMEM((tm, d), BF16)],
        compiler_params=_params(("parallel", "parallel")),
        name="inproj",
    )(x, mod, w)


def _sb_attention_kernel(q_ref, k_ref, v_ref, o_ref, qs_ref, carry_ref, acc_ref, *, blk):
    seq, width = q_ref.shape[1], q_ref.shape[2]
    n_pair = width // LANES
    n_head = 2 * n_pair
    scale = SB_HEAD_DIM ** -0.5

    lane = lax.broadcasted_iota(jnp.int32, (blk, LANES), 1)
    head_lanes = [lane < SB_HEAD_DIM, lane >= SB_HEAD_DIM]
    row = lax.broadcasted_iota(jnp.int32, (blk, blk), 0)
    col = lax.broadcasted_iota(jnp.int32, (blk, blk), 1)
    suffix = jnp.concatenate(
        [jnp.where(row > col, 1.0, 0.0), jnp.ones((blk, blk), F32)], axis=1).astype(BF16)
    srow = lax.broadcasted_iota(jnp.int32, (n_head * blk, blk), 0) & (blk - 1)
    scol = lax.broadcasted_iota(jnp.int32, (n_head * blk, blk), 1)
    diag_strict = scol < srow

    def key_block(k0, on_diagonal):
        z = jnp.concatenate(
            [_dot_nt(qs_ref[p * 2 * blk:(p + 1) * 2 * blk, :],
                     k_ref[0, pl.ds(k0, blk), p * LANES:(p + 1) * LANES]) for p in range(n_pair)],
            axis=0) * scale
        log_1m_beta = -(jnp.maximum(z, 0.0) + jnp.log(1.0 + jnp.exp(-jnp.abs(z))))
        if on_diagonal:
            log_1m_beta = jnp.where(diag_strict, log_1m_beta, 0.0)
        log_beta = log_1m_beta + z
        sums = _split_dot(log_1m_beta, suffix)
        c_old = carry_ref[...]
        a = jnp.exp(log_beta + sums[:, :blk] + c_old)
        if on_diagonal:
            a = jnp.where(diag_strict, a, 0.0)
        a = a.astype(BF16)
        carry_ref[...] = c_old + sums[:, blk:]
        for p in range(n_pair):
            v2 = v_ref[0, pl.ds(k0, blk), p * LANES:(p + 1) * LANES]
            upd = None
            for hh in range(2):
                h = 2 * p + hh
                vm = jnp.where(head_lanes[hh], v2, jnp.zeros_like(v2))
                t = _dot(a[h * blk:(h + 1) * blk], vm)
                upd = t if upd is None else upd + t
            acc_ref[p] += upd

    def all_underflowed():
        return (jnp.max(carry_ref[...]) < SB_UNDERFLOW_BOUND).astype(jnp.int32)

    n_peeled = 2

    def query_block(qi, _):
        q0 = pl.multiple_of(qi * blk, blk)
        carry_ref[...] = jnp.zeros_like(carry_ref)
        acc_ref[...] = jnp.zeros_like(acc_ref)
        for p in range(n_pair):
            q2 = q_ref[0, pl.ds(q0, blk), p * LANES:(p + 1) * LANES]
            for hh in range(2):
                h = 2 * p + hh
                qs_ref[h * blk:(h + 1) * blk, :] = jnp.where(head_lanes[hh], q2, jnp.zeros_like(q2))

        key_block(q0, True)
        for n in range(1, n_peeled + 1):
            @pl.when(qi >= n)
            def _():
                key_block(pl.multiple_of((qi - n) * blk, blk), False)

        def not_done(state):
            j, done = state
            return jnp.logical_and(j >= 0, done == 0)

        def sweep(state):
            j, _ = state
            key_block(pl.multiple_of(j * blk, blk), False)
            return j - 1, all_underflowed()

        lax.while_loop(not_done, sweep, (qi - (n_peeled + 1), all_underflowed()))
        for p in range(n_pair):
            o_ref[0, pl.ds(q0, blk), p * LANES:(p + 1) * LANES] = acc_ref[p].astype(o_ref.dtype)
        return 0

    lax.fori_loop(0, seq // blk, query_block, 0)


def _sb_attention(qkv, d):
    bsz, seq, _ = qkv.shape
    width = _pick(d, (256, 128))
    n_grp = d // width
    blk = 128
    spec = lambda off: pl.BlockSpec((1, seq, width), lambda b, g: (b, 0, off + g))
    return pl.pallas_call(
        functools.partial(_sb_attention_kernel, blk=blk),
        grid=(bsz, n_grp),
        in_specs=[spec(0), spec(n_grp), spec(2 * n_grp)],
        out_specs=pl.BlockSpec((1, seq, width), lambda b, g: (b, 0, g)),
        out_shape=jax.ShapeDtypeStruct((bsz, seq, d), BF16),
        scratch_shapes=[
            pltpu.VMEM((width // SB_HEAD_DIM * blk, LANES), BF16),
            pltpu.VMEM((width // SB_HEAD_DIM * blk, blk), F32),
            pltpu.VMEM((width // LANES, blk, LANES), F32),
        ],
        compiler_params=_params(("parallel", "parallel")),
        name="sb_attention",
    )(qkv, qkv, qkv)


def _hgrn_kernel(q_ref, f_ref, i_ref, g_ref, lb_ref, ng_ref, o_ref, state_ref):
    ts, d = q_ref.shape[1], q_ref.shape[2]
    n_heads = d // HG_HEAD_DIM
    c = HG_CHUNK
    mid = c // 2

    @pl.when(pl.program_id(1) == 0)
    def _():
        state_ref[...] = jnp.zeros_like(state_ref)

    row = lax.broadcasted_iota(jnp.int32, (c, c), 0)
    col = lax.broadcasted_iota(jnp.int32, (c, c), 1)
    causal = col <= row
    prefix = jnp.where(causal, 1.0, 0.0).astype(BF16)
    lb = lb_ref[...]
    one_m_lb = 1.0 - lb
    norm_g = jnp.concatenate([ng_ref[...]] * n_heads, axis=1)

    def chunk(ci, states):
        rows = pl.ds(pl.multiple_of(ci * c, c), c)
        qc = q_ref[0, rows, :].astype(F32)
        fc = f_ref[0, rows, :].astype(F32)
        vc = i_ref[0, rows, :]
        gc = g_ref[0, rows, :].astype(F32)
        e = jnp.exp(-jnp.abs(fc))
        big = 1.0 / (1.0 + e)
        small = e * big
        sig_pos = jnp.where(fc >= 0, big, small)
        sig_neg = jnp.where(fc >= 0, small, big)
        log_forget = jnp.log(lb + one_m_lb * sig_pos)
        kc = one_m_lb * sig_neg
        hi = log_forget.astype(BF16)
        lo = (log_forget - hi.astype(F32)).astype(BF16)
        cum = _dot(prefix, hi) + _dot(prefix, lo)
        g_mid = cum[mid - 1:mid]
        g_end = cum[c - 1:c]
        qd = (qc * jnp.exp(cum - g_mid)).astype(BF16)
        kd = (kc * jnp.exp(g_mid - cum)).astype(BF16)
        q_in = (qc * jnp.exp(cum)).astype(BF16)
        k_end = (kc * jnp.exp(g_end - cum)).astype(BF16)
        decay = jnp.exp(g_end)
        outs, new_states = [], []
        for h in range(n_heads):
            lanes = slice(h * HG_HEAD_DIM, (h + 1) * HG_HEAD_DIM)
            st = states[h]
            scores = jnp.where(causal, _dot_nt(qd[:, lanes], kd[:, lanes]), 0.0)
            o = _dot(scores.astype(BF16), vc[:, lanes]) + _dot_nt(q_in[:, lanes], st.astype(BF16))
            new_states.append(st * decay[:, lanes] + _dot_tn(vc[:, lanes], k_end[:, lanes]))
            ms = jnp.mean(o * o, axis=-1, keepdims=True)
            outs.append(o * lax.rsqrt(ms + RMS_EPS))
        y = jnp.concatenate(outs, axis=1) * norm_g * (gc * _sigmoid(gc))
        o_ref[0, rows, :] = y.astype(o_ref.dtype)
        return tuple(new_states)

    states = lax.fori_loop(0, ts // c, chunk, tuple(state_ref[h] for h in range(n_heads)))
    for h in range(n_heads):
        state_ref[h] = states[h]


def _hgrn(qfig, lb_row, norm_g, d):
    bsz, seq, _ = qfig.shape
    ts = _pick(seq, (512, 256, 128))
    n_heads = d // HG_HEAD_DIM
    spec = lambda off: pl.BlockSpec((1, ts, d), lambda b, s: (b, s, off))
    return pl.pallas_call(
        _hgrn_kernel,
        grid=(bsz, seq // ts),
        in_specs=[spec(0), spec(1), spec(2), spec(3),
                  pl.BlockSpec((1, d), lambda b, s: (0, 0)),
                  pl.BlockSpec((1, HG_HEAD_DIM), lambda b, s: (0, 0))],
        out_specs=pl.BlockSpec((1, ts, d), lambda b, s: (b, s, 0)),
        out_shape=jax.ShapeDtypeStruct((bsz, seq, d), BF16),
        scratch_shapes=[pltpu.VMEM((n_heads, HG_HEAD_DIM, HG_HEAD_DIM), F32)],
        compiler_params=_params(("parallel", "arbitrary")),
        name="hgrn",
    )(qfig, qfig, qfig, qfig, lb_row.reshape(1, d), norm_g.reshape(1, HG_HEAD_DIM).astype(F32))


def _outproj_ln_kernel(y_ref, w_ref, x_ref, m_ref, g_ref, b_ref, o_ref, *, alpha):
    y = _dot(y_ref[0], w_ref[...])
    gate = m_ref[0][GATE1:GATE1 + 1]
    o_ref[0] = _layer_norm(alpha * x_ref[0] + gate * y, g_ref[...], b_ref[...])


def _outproj_ln(y, w, x, mod, ln_g, ln_b, alpha):
    bsz, seq, d = x.shape
    tm = _pick(seq, (512, 256, 128))
    tok = pl.BlockSpec((1, tm, d), lambda b, i: (b, i, 0))
    vec = pl.BlockSpec((1, d), lambda b, i: (0, 0))
    return pl.pallas_call(
        functools.partial(_outproj_ln_kernel, alpha=alpha),
        grid=(bsz, seq // tm),
        in_specs=[tok, pl.BlockSpec((d, d), lambda b, i: (0, 0)), tok,
                  pl.BlockSpec((1, 6, d), lambda b, i: (b, 0, 0)), vec, vec],
        out_specs=tok,
        out_shape=jax.ShapeDtypeStruct((bsz, seq, d), F32),
        compiler_params=_params(("parallel", "parallel")),
        name="outproj_ln",
    )(y, w, x, mod, ln_g.reshape(1, d), ln_b.reshape(1, d))


def _ffn_kernel(x_ref, m_ref, wa_ref, wb_ref, w2_ref, g_ref, b_ref, o_ref, h_ref, acc_ref, *, alpha):
    f = pl.program_id(2)
    m = m_ref[0]

    @pl.when(f == 0)
    def _():
        h_ref[...] = (x_ref[0] * (1.0 + m[SCALE2:SCALE2 + 1]) + m[SHIFT2:SHIFT2 + 1]).astype(BF16)
        acc_ref[...] = jnp.zeros_like(acc_ref)

    h = h_ref[...]
    a = _dot(h, wa_ref[...])
    b = _dot(h, wb_ref[...])
    u = a * _sigmoid(a) * b
    acc_ref[...] += _dot(u.astype(BF16), w2_ref[...])

    @pl.when(f == pl.num_programs(2) - 1)
    def _():
        r = alpha * x_ref[0] + m[GATE2:GATE2 + 1] * acc_ref[...]
        o_ref[0] = _layer_norm(r, g_ref[...], b_ref[...])


def _ffn(x, mod, w13, w2, ln_g, ln_b, alpha):
    bsz, seq, d = x.shape
    ff = w2.shape[0]
    tm = _pick(seq, (1024, 512, 256, 128))
    tf = _pick(ff, (512, 256, 128))
    nf = ff // tf
    tok = pl.BlockSpec((1, tm, d), lambda b, i, f: (b, i, 0))
    vec = pl.BlockSpec((1, d), lambda b, i, f: (0, 0))
    return pl.pallas_call(
        functools.partial(_ffn_kernel, alpha=alpha),
        grid=(bsz, seq // tm, nf),
        in_specs=[tok, pl.BlockSpec((1, 6, d), lambda b, i, f: (b, 0, 0)),
                  pl.BlockSpec((d, tf), lambda b, i, f: (0, f)),
                  pl.BlockSpec((d, tf), lambda b, i, f: (0, nf + f)),
                  pl.BlockSpec((tf, d), lambda b, i, f: (f, 0)),
                  vec, vec],
        out_specs=tok,
        out_shape=jax.ShapeDtypeStruct((bsz, seq, d), F32),
        scratch_shapes=[pltpu.VMEM((tm, d), BF16), pltpu.VMEM((tm, d), F32)],
        compiler_params=_params(("parallel", "parallel", "arbitrary")),
        name="ffn",
    )(x, mod, w13, w13, w2, ln_g.reshape(1, d), ln_b.reshape(1, d))


def _route_kernel(x_ref, m_ref, rt_ref, h_ref, gates_ref, rank_ref, cnt_ref, before_ref, *, n_experts):
    tm = x_ref.shape[1]
    n_rows = rt_ref.shape[0]

    @pl.when(jnp.logical_and(pl.program_id(0) == 0, pl.program_id(1) == 0))
    def _():
        step = 256 if tm % 256 == 0 else tm
        for r in range(0, tm, step):
            s = lax.broadcasted_iota(jnp.int32, (step, tm), 0) + r
            t = lax.broadcasted_iota(jnp.int32, (step, tm), 1)
            before_ref[r:r + step, :] = jnp.where(s < t, 1.0, 0.0).astype(BF16)

    m = m_ref[0]
    h = (x_ref[0] * (1.0 + m[SCALE2:SCALE2 + 1]) + m[SHIFT2:SHIFT2 + 1]).astype(BF16)
    h_ref[0] = h
    logits = _dot_nt(rt_ref[...], h)
    sub = lax.broadcasted_iota(jnp.int32, logits.shape, 0)
    logits = jnp.where(sub < n_experts, logits, -jnp.inf)
    m1 = jnp.max(logits, axis=0, keepdims=True)
    i1 = jnp.min(jnp.where(logits == m1, sub, n_rows), axis=0, keepdims=True)
    rest = jnp.where(sub == i1, -jnp.inf, logits)
    m2 = jnp.max(rest, axis=0, keepdims=True)
    i2 = jnp.min(jnp.where(rest == m2, sub, n_rows), axis=0, keepdims=True)
    t = jnp.exp(m2 - m1)
    w1 = 1.0 / (1.0 + t)
    first, second = sub == i1, sub == i2
    routed = jnp.logical_or(first, second)
    ind = jnp.where(routed, 1.0, 0.0).astype(F32)
    rank = _dot(ind.astype(BF16), before_ref[...])
    gates_ref[0, 0] = jnp.where(first, w1, 0.0) + jnp.where(second, t * w1, 0.0)
    rank_ref[0, 0] = jnp.where(routed, rank, -1.0)
    cnt_ref[0, 0] = jnp.broadcast_to(jnp.sum(ind, axis=1, keepdims=True), (n_rows, LANES))


def _route(x, mod, router_w, tm):
    bsz, seq, d = x.shape
    n_experts = router_w.shape[1]
    n_rows = -(-n_experts // 8) * 8
    nt = seq // tm
    rt = jnp.zeros((n_rows, d), BF16).at[:n_experts].set(router_w.T.astype(BF16))
    per_tile = lambda w: pl.BlockSpec((1, 1, n_rows, w), lambda b, i: (b, i, 0, 0))
    return pl.pallas_call(
        functools.partial(_route_kernel, n_experts=n_experts),
        grid=(bsz, nt),
        in_specs=[pl.BlockSpec((1, tm, d), lambda b, i: (b, i, 0)),
                  pl.BlockSpec((1, 6, d), lambda b, i: (b, 0, 0)),
                  pl.BlockSpec((n_rows, d), lambda b, i: (0, 0))],
        out_specs=[pl.BlockSpec((1, tm, d), lambda b, i: (b, i, 0)),
                   per_tile(tm), per_tile(tm), per_tile(LANES)],
        out_shape=[jax.ShapeDtypeStruct((bsz, seq, d), BF16),
                   jax.ShapeDtypeStruct((bsz, nt, n_rows, tm), F32),
                   jax.ShapeDtypeStruct((bsz, nt, n_rows, tm), F32),
                   jax.ShapeDtypeStruct((bsz, nt, n_rows, LANES), F32)],
        scratch_shapes=[pltpu.VMEM((tm, tm), BF16)],
        compiler_params=_params(("arbitrary", "arbitrary")),
        name="route",
    )(x, mod, rt)


def _moe_kernel(cnt_ref, h_ref, gates_ref, rank_ref, wa_ref, wb_ref, w2_ref, y_ref, hc_ref, acc_ref,
                *, big, small):
    b, i, e, f = (pl.program_id(k) for k in range(4))
    n_tiles, n_exp, nf = (pl.num_programs(k) for k in (1, 2, 3))
    tm = h_ref.shape[1]
    n = cnt_ref[(b * n_tiles + i) * n_exp + e]
    n_big = n // big
    n_small = (n - n_big * big + small - 1) // small
    rank_e = rank_ref[0, 0, pl.ds(e, 1), :]
    gate_e = gates_ref[0, 0, pl.ds(e, 1), :]

    def for_chunks(fn):
        def big_step(c, carry):
            fn(pl.multiple_of(c * big, big), big)
            return carry

        def small_step(c, carry):
            fn(pl.multiple_of(n_big * big + c * small, small), small)
            return carry

        lax.fori_loop(0, n_big, big_step, 0)
        lax.fori_loop(0, n_small, small_step, 0)

    def selected(base, m):
        r = (lax.broadcasted_iota(jnp.int32, (m, tm), 0) + base).astype(F32)
        return rank_e == r

    @pl.when(jnp.logical_and(e == 0, f == 0))
    def _():
        y_ref[...] = jnp.zeros_like(y_ref)

    @pl.when(f == 0)
    def _():
        def gather(base, m):
            p = jnp.where(selected(base, m), 1.0, 0.0).astype(BF16)
            hc_ref[pl.ds(base, m), :] = _dot(p, h_ref[0]).astype(BF16)
            acc_ref[pl.ds(base, m), :] = jnp.zeros((m, acc_ref.shape[1]), F32)
        for_chunks(gather)

    def expert(base, m):
        rows = pl.ds(base, m)
        hc = hc_ref[rows, :]
        a = _dot(hc, wa_ref[0])
        g = _dot(hc, wb_ref[0])
        u = a * _sigmoid(a) * g
        acc_ref[rows, :] += _dot(u.astype(BF16), w2_ref[0])
    for_chunks(expert)

    @pl.when(f == nf - 1)
    def _():
        def scatter(base, m):
            sel = selected(base, m)
            gate_rows = jnp.sum(jnp.where(sel, gate_e, 0.0), axis=1, keepdims=True)
            val = (acc_ref[pl.ds(base, m), :] * gate_rows).astype(BF16)
            y_ref[0] += _dot_tn(jnp.where(sel, 1.0, 0.0).astype(BF16), val)
        for_chunks(scatter)


def _moe(h, gates, rank, counts, w13, w2, tm):
    bsz, seq, d = h.shape
    n_exp, ff = w2.shape[0], w2.shape[1]
    n_rows = gates.shape[2]
    tf = _pick(ff, (512, 256, 128))
    nf = ff // tf
    big = tm // 4
    small = big // 4
    per_tile = pl.BlockSpec((1, 1, n_rows, tm), lambda b, i, e, f, cnt: (b, i, 0, 0))
    tok = pl.BlockSpec((1, tm, d), lambda b, i, e, f, cnt: (b, i, 0))
    return pl.pallas_call(
        functools.partial(_moe_kernel, big=big, small=small),
        grid_spec=pltpu.PrefetchScalarGridSpec(
            num_scalar_prefetch=1,
            grid=(bsz, seq // tm, n_exp, nf),
            in_specs=[tok, per_tile, per_tile,
                      pl.BlockSpec((1, d, tf), lambda b, i, e, f, cnt: (e, 0, f)),
                      pl.BlockSpec((1, d, tf), lambda b, i, e, f, cnt: (e, 0, nf + f)),
                      pl.BlockSpec((1, tf, d), lambda b, i, e, f, cnt: (e, f, 0))],
            out_specs=tok,
            scratch_shapes=[pltpu.VMEM((tm, d), BF16), pltpu.VMEM((tm, d), F32)],
        ),
        out_shape=jax.ShapeDtypeStruct((bsz, seq, d), F32),
        compiler_params=_params(("parallel", "parallel", "arbitrary", "arbitrary")),
        name="moe",
    )(counts, h, gates, rank, w13, w13, w2)


def _residual_ln_kernel(x_ref, y_ref, m_ref, g_ref, b_ref, o_ref, *, alpha):
    gate = m_ref[0][GATE2:GATE2 + 1]
    o_ref[0] = _layer_norm(alpha * x_ref[0] + gate * y_ref[0], g_ref[...], b_ref[...])


def _residual_ln(x, y, mod, ln_g, ln_b, alpha):
    bsz, seq, d = x.shape
    tm = _pick(seq, (512, 256, 128))
    tok = pl.BlockSpec((1, tm, d), lambda b, i: (b, i, 0))
    vec = pl.BlockSpec((1, d), lambda b, i: (0, 0))
    return pl.pallas_call(
        functools.partial(_residual_ln_kernel, alpha=alpha),
        grid=(bsz, seq // tm),
        in_specs=[tok, tok, pl.BlockSpec((1, 6, d), lambda b, i: (b, 0, 0)), vec, vec],
        out_specs=tok,
        out_shape=jax.ShapeDtypeStruct((bsz, seq, d), F32),
        compiler_params=_params(("parallel", "parallel")),
        name="residual_ln",
    )(x, y, mod, ln_g.reshape(1, d), ln_b.reshape(1, d))


def _moe_layer(x, mod, router_w, w13, w2, ln_g, ln_b, alpha):
    bsz, seq, d = x.shape
    tm = _pick(seq, (2048, 1024, 512, 256, 128))
    h, gates, rank, cnt = _route(x, mod, router_w, tm)
    counts = cnt[:, :, :w2.shape[0], 0].astype(jnp.int32).reshape(-1)
    y = _moe(h, gates, rank, counts, w13, w2, tm)
    return _residual_ln(x, y, mod, ln_g, ln_b, alpha)


def kernel(x, c, ada_w, ada_b, ln1_g, ln1_b, ln2_g, ln2_b, sb_w_qkv, sb_w_o, hg_w_in, hg_norm_g,
           hg_w_o, hg_lb_logits, ffn_w13, ffn_w2, moe_router, moe_w13, moe_w2):
    bsz, seq, d = x.shape
    depth = ada_w.shape[0]
    alpha = (2.0 * depth) ** 0.25

    lb_table = _lb_table(hg_lb_logits)
    mod_all = _modulation(c, ada_w, ada_b).reshape(depth, bsz, 6, d)

    for layer in range(depth):
        mod = mod_all[layer]
        j = layer // 2
        if layer % 2 == 0:
            qkv = _inproj(x, mod, sb_w_qkv[j].astype(BF16))
            y = _sb_attention(qkv, d)
            x = _outproj_ln(y, sb_w_o[j].astype(BF16), x, mod, ln1_g[layer], ln1_b[layer], alpha)
            x = _ffn(x, mod, ffn_w13[j].astype(BF16), ffn_w2[j].astype(BF16),
                     ln2_g[layer], ln2_b[layer], alpha)
        else:
            qfig = _inproj(x, mod, hg_w_in[j].astype(BF16))
            y = _hgrn(qfig, lb_table[layer], hg_norm_g[j], d)
            x = _outproj_ln(y, hg_w_o[j].astype(BF16), x, mod, ln1_g[layer], ln1_b[layer], alpha)
            x = _moe_layer(x, mod, moe_router[j], moe_w13[j].astype(BF16), moe_w2[j].astype(BF16),
                           ln2_g[layer], ln2_b[layer], alpha)
    return x
```

```python
import functools
import math

import jax
import jax.numpy as jnp
from jax import lax
from jax.experimental import pallas as pl
from jax.experimental.pallas import tpu as pltpu

F32 = jnp.float32
BF16 = jnp.bfloat16

LANES = 128
SB_HEAD_DIM = 64
HG_HEAD_DIM = 128
HG_CHUNK = 32
LN_EPS = 1e-5
RMS_EPS = 1e-6
VMEM_LIMIT_BYTES = 56 * 1024 * 1024
SB_UNDERFLOW_BOUND = -105.0

SHIFT1, SCALE1, GATE1, SHIFT2, SCALE2, GATE2 = range(6)


def _params(sem):
    return pltpu.CompilerParams(dimension_semantics=sem, vmem_limit_bytes=VMEM_LIMIT_BYTES)


def _pick(n, candidates):
    for c in candidates:
        if n % c == 0:
            return c
    raise ValueError(f"no tile in {candidates} divides {n}")


def _sigmoid(x):
    return 1.0 / (1.0 + jnp.exp(-x))


def _dot(a, b):
    return jnp.dot(a, b, preferred_element_type=F32)


def _dot_nt(a, b):
    return lax.dot_general(a, b, (((1,), (1,)), ((), ())), preferred_element_type=F32)


def _dot_tn(a, b):
    return lax.dot_general(a, b, (((0,), (0,)), ((), ())), preferred_element_type=F32)


def _split_dot(x, m):
    hi = x.astype(BF16)
    lo = (x - hi.astype(F32)).astype(BF16)
    return _dot(hi, m) + _dot(lo, m)


def _layer_norm(r, g, b):
    mu = jnp.mean(r, axis=-1, keepdims=True)
    d = r - mu
    var = jnp.mean(d * d, axis=-1, keepdims=True)
    return d * lax.rsqrt(var + LN_EPS) * g + b


def _lb_table_kernel(g_ref, o_ref):
    g = g_ref[...]
    e = jnp.exp(g - jnp.max(g, axis=0, keepdims=True))
    p = e / jnp.sum(e, axis=0, keepdims=True)
    run = jnp.zeros_like(p[0:1])
    for l in range(g.shape[0]):
        run = run + p[l:l + 1]
        o_ref[l:l + 1, :] = run - p[0:1]


def _lb_table(logits):
    return pl.pallas_call(
        _lb_table_kernel,
        out_shape=jax.ShapeDtypeStruct(logits.shape, F32),
        name="lb_table",
    )(logits.astype(F32))


def _modulation_kernel(c_ref, w_ref, b_ref, o_ref):
    c = c_ref[...]
    cond = c * _sigmoid(c)
    y = jnp.dot(cond, w_ref[0], preferred_element_type=F32, precision=lax.Precision.HIGHEST)
    o_ref[0] = y + b_ref[0]


def _modulation(c, ada_w, ada_b):
    depth, d, n = ada_w.shape
    bsz = c.shape[0]
    tn = _pick(n, (1536, 1024, 512, 256, 128))
    return pl.pallas_call(
        _modulation_kernel,
        grid=(depth, n // tn),
        in_specs=[
            pl.BlockSpec((bsz, d), lambda l, j: (0, 0)),
            pl.BlockSpec((1, d, tn), lambda l, j: (l, 0, j)),
            pl.BlockSpec((1, 1, tn), lambda l, j: (l, 0, j)),
        ],
        out_specs=pl.BlockSpec((1, bsz, tn), lambda l, j: (l, 0, j)),
        out_shape=jax.ShapeDtypeStruct((depth, bsz, n), F32),
        compiler_params=_params(("parallel", "parallel")),
        name="modulation",
    )(c, ada_w, ada_b.reshape(depth, 1, n))


def _inproj_kernel(x_ref, m_ref, w_ref, o_ref, h_ref, *, n_chunk):
    m = m_ref[0]
    h_ref[...] = (x_ref[0] * (1.0 + m[SCALE1:SCALE1 + 1]) + m[SHIFT1:SHIFT1 + 1]).astype(BF16)
    for n in range(w_ref.shape[1] // n_chunk):
        cols = slice(n * n_chunk, (n + 1) * n_chunk)
        o_ref[0, :, cols] = _dot(h_ref[...], w_ref[:, cols]).astype(o_ref.dtype)


def _inproj(x, mod, w):
    bsz, seq, d = x.shape
    n = w.shape[1]
    tm = _pick(seq, (512, 256, 128))
    n_chunk = _pick(n, (512, 256, 128))
    return pl.pallas_call(
        functools.partial(_inproj_kernel, n_chunk=n_chunk),
        grid=(bsz, seq // tm),
        in_specs=[
            pl.BlockSpec((1, tm, d), lambda b, i: (b, i, 0)),
            pl.BlockSpec((1, 6, d), lambda b, i: (b, 0, 0)),
            pl.BlockSpec((d, n), lambda b, i: (0, 0)),
        ],
        out_specs=pl.BlockSpec((1, tm, n), lambda b, i: (b, i, 0)),
        out_shape=jax.ShapeDtypeStruct((bsz, seq, n), BF16),
        scratch_shapes=[pltpu.VMEM((tm, d), BF16)],
        compiler_params=_params(("parallel", "parallel")),
        name="inproj",
    )(x, mod, w)


def _sb_attention_kernel(q_ref, k_ref, v_ref, o_ref, qs_ref, carry_ref, acc_ref, *, blk):
    seq, width = q_ref.shape[1], q_ref.shape[2]
    n_pair = width // LANES
    n_head = 2 * n_pair
    scale = SB_HEAD_DIM ** -0.5

    lane = lax.broadcasted_iota(jnp.int32, (blk, LANES), 1)
    head_lanes = [lane < SB_HEAD_DIM, lane >= SB_HEAD_DIM]
    row = lax.broadcasted_iota(jnp.int32, (blk, blk), 0)
    col = lax.broadcasted_iota(jnp.int32, (blk, blk), 1)
    suffix = jnp.concatenate(
        [jnp.where(row > col, 1.0, 0.0), jnp.ones((blk, blk), F32)], axis=1).astype(BF16)
    srow = lax.broadcasted_iota(jnp.int32, (n_head * blk, blk), 0) & (blk - 1)
    scol = lax.broadcasted_iota(jnp.int32, (n_head * blk, blk), 1)
    diag_strict = scol < srow

    def key_block(k0, on_diagonal):
        z = jnp.concatenate(
            [_dot_nt(qs_ref[p * 2 * blk:(p + 1) * 2 * blk, :],
                     k_ref[0, pl.ds(k0, blk), p * LANES:(p + 1) * LANES]) for p in range(n_pair)],
            axis=0) * scale
        log_1m_beta = -(jnp.maximum(z, 0.0) + jnp.log(1.0 + jnp.exp(-jnp.abs(z))))
        if on_diagonal:
            log_1m_beta = jnp.where(diag_strict, log_1m_beta, 0.0)
        log_beta = log_1m_beta + z
        sums = _split_dot(log_1m_beta, suffix)
        c_old = carry_ref[...]
        a = jnp.exp(log_beta + sums[:, :blk] + c_old)
        if on_diagonal:
            a = jnp.where(diag_strict, a, 0.0)
        a = a.astype(BF16)
        carry_ref[...] = c_old + sums[:, blk:]
        for p in range(n_pair):
            v2 = v_ref[0, pl.ds(k0, blk), p * LANES:(p + 1) * LANES]
            upd = None
            for hh in range(2):
                h = 2 * p + hh
                vm = jnp.where(head_lanes[hh], v2, jnp.zeros_like(v2))
                t = _dot(a[h * blk:(h + 1) * blk], vm)
                upd = t if upd is None else upd + t
            acc_ref[p] += upd

    def all_underflowed():
        return (jnp.max(carry_ref[...]) < SB_UNDERFLOW_BOUND).astype(jnp.int32)

    n_peeled = 2

    def query_block(qi, _):
        q0 = pl.multiple_of(qi * blk, blk)
        carry_ref[...] = jnp.zeros_like(carry_ref)
        acc_ref[...] = jnp.zeros_like(acc_ref)
        for p in range(n_pair):
            q2 = q_ref[0, pl.ds(q0, blk), p * LANES:(p + 1) * LANES]
            for hh in range(2):
                h = 2 * p + hh
                qs_ref[h * blk:(h + 1) * blk, :] = jnp.where(head_lanes[hh], q2, jnp.zeros_like(q2))

        key_block(q0, True)
        for n in range(1, n_peeled + 1):
            @pl.when(qi >= n)
            def _():
                key_block(pl.multiple_of((qi - n) * blk, blk), False)

        def not_done(state):
            j, done = state
            return jnp.logical_and(j >= 0, done == 0)

        def sweep(state):
            j, _ = state
            key_block(pl.multiple_of(j * blk, blk), False)
            return j - 1, all_underflowed()

        lax.while_loop(not_done, sweep, (qi - (n_peeled + 1), all_underflowed()))
        for p in range(n_pair):
            o_ref[0, pl.ds(q0, blk), p * LANES:(p + 1) * LANES] = acc_ref[p].astype(o_ref.dtype)
        return 0

    lax.fori_loop(0, seq // blk, query_block, 0)


def _sb_attention(qkv, d):
    bsz, seq, _ = qkv.shape
    width = _pick(d, (256, 128))
    n_grp = d // width
    blk = 128
    spec = lambda off: pl.BlockSpec((1, seq, width), lambda b, g: (b, 0, off + g))
    return pl.pallas_call(
        functools.partial(_sb_attention_kernel, blk=blk),
        grid=(bsz, n_grp),
        in_specs=[spec(0), spec(n_grp), spec(2 * n_grp)],
        out_specs=pl.BlockSpec((1, seq, width), lambda b, g: (b, 0, g)),
        out_shape=jax.ShapeDtypeStruct((bsz, seq, d), BF16),
        scratch_shapes=[
            pltpu.VMEM((width // SB_HEAD_DIM * blk, LANES), BF16),
            pltpu.VMEM((width // SB_HEAD_DIM * blk, blk), F32),
            pltpu.VMEM((width // LANES, blk, LANES), F32),
        ],
        compiler_params=_params(("parallel", "parallel")),
        name="sb_attention",
    )(qkv, qkv, qkv)


def _hgrn_kernel(q_ref, f_ref, i_ref, g_ref, lb_ref, ng_ref, o_ref, state_ref):
    ts, d = q_ref.shape[1], q_ref.shape[2]
    n_heads = d // HG_HEAD_DIM
    c = HG_CHUNK
    mid = c // 2

    @pl.when(pl.program_id(1) == 0)
    def _():
        state_ref[...] = jnp.zeros_like(state_ref)

    row = lax.broadcasted_iota(jnp.int32, (c, c), 0)
    col = lax.broadcasted_iota(jnp.int32, (c, c), 1)
    causal = col <= row
    prefix = jnp.where(causal, 1.0, 0.0).astype(BF16)
    lb = lb_ref[...]
    one_m_lb = 1.0 - lb
    norm_g = jnp.concatenate([ng_ref[...]] * n_heads, axis=1)

    def chunk(ci, states):
        rows = pl.ds(pl.multiple_of(ci * c, c), c)
        qc = q_ref[0, rows, :].astype(F32)
        fc = f_ref[0, rows, :].astype(F32)
        vc = i_ref[0, rows, :]
        gc = g_ref[0, rows, :].astype(F32)
        e = jnp.exp(-jnp.abs(fc))
        big = 1.0 / (1.0 + e)
        small = e * big
        sig_pos = jnp.where(fc >= 0, big, small)
        sig_neg = jnp.where(fc >= 0, small, big)
        log_forget = jnp.log(lb + one_m_lb * sig_pos)
        kc = one_m_lb * sig_neg
        hi = log_forget.astype(BF16)
        lo = (log_forget - hi.astype(F32)).astype(BF16)
        cum = _dot(prefix, hi) + _dot(prefix, lo)
        g_mid = cum[mid - 1:mid]
        g_end = cum[c - 1:c]
        qd = (qc * jnp.exp(cum - g_mid)).astype(BF16)
        kd = (kc * jnp.exp(g_mid - cum)).astype(BF16)
        q_in = (qc * jnp.exp(cum)).astype(BF16)
        k_end = (kc * jnp.exp(g_end - cum)).astype(BF16)
        decay = jnp.exp(g_end)
        outs, new_states = [], []
        for h in range(n_heads):
            lanes = slice(h * HG_HEAD_DIM, (h + 1) * HG_HEAD_DIM)
            st = states[h]
            scores = jnp.where(causal, _dot_nt(qd[:, lanes], kd[:, lanes]), 0.0)
            o = _dot(scores.astype(BF16), vc[:, lanes]) + _dot_nt(q_in[:, lanes], st.astype(BF16))
            new_states.append(st * decay[:, lanes] + _dot_tn(vc[:, lanes], k_end[:, lanes]))
            ms = jnp.mean(o * o, axis=-1, keepdims=True)
            outs.append(o * lax.rsqrt(ms + RMS_EPS))
        y = jnp.concatenate(outs, axis=1) * norm_g * (gc * _sigmoid(gc))
        o_ref[0, rows, :] = y.astype(o_ref.dtype)
        return tuple(new_states)

    states = lax.fori_loop(0, ts // c, chunk, tuple(state_ref[h] for h in range(n_heads)), unroll=2)
    for h in range(n_heads):
        state_ref[h] = states[h]


def _hgrn(qfig, lb_row, norm_g, d):
    bsz, seq, _ = qfig.shape
    ts = _pick(seq, (512, 256, 128))
    n_heads = d // HG_HEAD_DIM
    spec = lambda off: pl.BlockSpec((1, ts, d), lambda b, s: (b, s, off))
    return pl.pallas_call(
        _hgrn_kernel,
        grid=(bsz, seq // ts),
        in_specs=[spec(0), spec(1), spec(2), spec(3),
                  pl.BlockSpec((1, d), lambda b, s: (0, 0)),
                  pl.BlockSpec((1, HG_HEAD_DIM), lambda b, s: (0, 0))],
        out_specs=pl.BlockSpec((1, ts, d), lambda b, s: (b, s, 0)),
        out_shape=jax.ShapeDtypeStruct((bsz, seq, d), BF16),
        scratch_shapes=[pltpu.VMEM((n_heads, HG_HEAD_DIM, HG_HEAD_DIM), F32)],
        compiler_params=_params(("parallel", "arbitrary")),
        name="hgrn",
    )(qfig, qfig, qfig, qfig, lb_row.reshape(1, d), norm_g.reshape(1, HG_HEAD_DIM).astype(F32))


def _outproj_ln_kernel(y_ref, w_ref, x_ref, m_ref, g_ref, b_ref, o_ref, *, alpha):
    y = _dot(y_ref[0], w_ref[...])
    gate = m_ref[0][GATE1:GATE1 + 1]
    o_ref[0] = _layer_norm(alpha * x_ref[0] + gate * y, g_ref[...], b_ref[...])


def _outproj_ln(y, w, x, mod, ln_g, ln_b, alpha):
    bsz, seq, d = x.shape
    tm = _pick(seq, (512, 256, 128))
    tok = pl.BlockSpec((1, tm, d), lambda b, i: (b, i, 0))
    vec = pl.BlockSpec((1, d), lambda b, i: (0, 0))
    return pl.pallas_call(
        functools.partial(_outproj_ln_kernel, alpha=alpha),
        grid=(bsz, seq // tm),
        in_specs=[tok, pl.BlockSpec((d, d), lambda b, i: (0, 0)), tok,
                  pl.BlockSpec((1, 6, d), lambda b, i: (b, 0, 0)), vec, vec],
        out_specs=tok,
        out_shape=jax.ShapeDtypeStruct((bsz, seq, d), F32),
        compiler_params=_params(("parallel", "parallel")),
        name="outproj_ln",
    )(y, w, x, mod, ln_g.reshape(1, d), ln_b.reshape(1, d))


def _ffn_kernel(x_ref, m_ref, wa_ref, wb_ref, w2_ref, g_ref, b_ref, o_ref, h_ref, acc_ref, *, alpha):
    f = pl.program_id(2)
    m = m_ref[0]

    @pl.when(f == 0)
    def _():
        h_ref[...] = (x_ref[0] * (1.0 + m[SCALE2:SCALE2 + 1]) + m[SHIFT2:SHIFT2 + 1]).astype(BF16)
        acc_ref[...] = jnp.zeros_like(acc_ref)

    h = h_ref[...]
    a = _dot(h, wa_ref[...])
    b = _dot(h, wb_ref[...])
    u = a * _sigmoid(a) * b
    acc_ref[...] += _dot(u.astype(BF16), w2_ref[...])

    @pl.when(f == pl.num_programs(2) - 1)
    def _():
        r = alpha * x_ref[0] + m[GATE2:GATE2 + 1] * acc_ref[...]
        o_ref[0] = _layer_norm(r, g_ref[...], b_ref[...])


def _ffn(x, mod, w13, w2, ln_g, ln_b, alpha):
    bsz, seq, d = x.shape
    ff = w2.shape[0]
    tm = _pick(seq, (1024, 512, 256, 128))
    tf = _pick(ff, (512, 256, 128))
    nf = ff // tf
    tok = pl.BlockSpec((1, tm, d), lambda b, i, f: (b, i, 0))
    vec = pl.BlockSpec((1, d), lambda b, i, f: (0, 0))
    return pl.pallas_call(
        functools.partial(_ffn_kernel, alpha=alpha),
        grid=(bsz, seq // tm, nf),
        in_specs=[tok, pl.BlockSpec((1, 6, d), lambda b, i, f: (b, 0, 0)),
                  pl.BlockSpec((d, tf), lambda b, i, f: (0, f)),
                  pl.BlockSpec((d, tf), lambda b, i, f: (0, nf + f)),
                  pl.BlockSpec((tf, d), lambda b, i, f: (f, 0)),
                  vec, vec],
        out_specs=tok,
        out_shape=jax.ShapeDtypeStruct((bsz, seq, d), F32),
        scratch_shapes=[pltpu.VMEM((tm, d), BF16), pltpu.VMEM((tm, d), F32)],
        compiler_params=_params(("parallel", "parallel", "arbitrary")),
        name="ffn",
    )(x, mod, w13, w13, w2, ln_g.reshape(1, d), ln_b.reshape(1, d))


def _route_kernel(x_ref, m_ref, rt_ref, h_ref, gates_ref, rank_ref, cnt_ref, before_ref, *, n_experts):
    tm = x_ref.shape[1]
    n_rows = rt_ref.shape[0]

    @pl.when(jnp.logical_and(pl.program_id(0) == 0, pl.program_id(1) == 0))
    def _():
        step = 256 if tm % 256 == 0 else tm
        for r in range(0, tm, step):
            s = lax.broadcasted_iota(jnp.int32, (step, tm), 0) + r
            t = lax.broadcasted_iota(jnp.int32, (step, tm), 1)
            before_ref[r:r + step, :] = jnp.where(s < t, 1.0, 0.0).astype(BF16)

    m = m_ref[0]
    h = (x_ref[0] * (1.0 + m[SCALE2:SCALE2 + 1]) + m[SHIFT2:SHIFT2 + 1]).astype(BF16)
    h_ref[0] = h
    logits = _dot_nt(rt_ref[...], h)
    sub = lax.broadcasted_iota(jnp.int32, logits.shape, 0)
    logits = jnp.where(sub < n_experts, logits, -jnp.inf)
    m1 = jnp.max(logits, axis=0, keepdims=True)
    i1 = jnp.min(jnp.where(logits == m1, sub, n_rows), axis=0, keepdims=True)
    rest = jnp.where(sub == i1, -jnp.inf, logits)
    m2 = jnp.max(rest, axis=0, keepdims=True)
    i2 = jnp.min(jnp.where(rest == m2, sub, n_rows), axis=0, keepdims=True)
    t = jnp.exp(m2 - m1)
    w1 = 1.0 / (1.0 + t)
    first, second = sub == i1, sub == i2
    routed = jnp.logical_or(first, second)
    ind = jnp.where(routed, 1.0, 0.0).astype(F32)
    rank = _dot(ind.astype(BF16), before_ref[...])
    gates_ref[0, 0] = jnp.where(first, w1, 0.0) + jnp.where(second, t * w1, 0.0)
    rank_ref[0, 0] = jnp.where(routed, rank, -1.0)
    cnt_ref[0, 0] = jnp.broadcast_to(jnp.sum(ind, axis=1, keepdims=True), (n_rows, LANES))


def _route(x, mod, router_w, tm):
    bsz, seq, d = x.shape
    n_experts = router_w.shape[1]
    n_rows = -(-n_experts // 8) * 8
    nt = seq // tm
    rt = jnp.zeros((n_rows, d), BF16).at[:n_experts].set(router_w.T.astype(BF16))
    per_tile = lambda w: pl.BlockSpec((1, 1, n_rows, w), lambda b, i: (b, i, 0, 0))
    return pl.pallas_call(
        functools.partial(_route_kernel, n_experts=n_experts),
        grid=(bsz, nt),
        in_specs=[pl.BlockSpec((1, tm, d), lambda b, i: (b, i, 0)),
                  pl.BlockSpec((1, 6, d), lambda b, i: (b, 0, 0)),
                  pl.BlockSpec((n_rows, d), lambda b, i: (0, 0))],
        out_specs=[pl.BlockSpec((1, tm, d), lambda b, i: (b, i, 0)),
                   per_tile(tm), per_tile(tm), per_tile(LANES)],
        out_shape=[jax.ShapeDtypeStruct((bsz, seq, d), BF16),
                   jax.ShapeDtypeStruct((bsz, nt, n_rows, tm), F32),
                   jax.ShapeDtypeStruct((bsz, nt, n_rows, tm), F32),
                   jax.ShapeDtypeStruct((bsz, nt, n_rows, LANES), F32)],
        scratch_shapes=[pltpu.VMEM((tm, tm), BF16)],
        compiler_params=_params(("arbitrary", "arbitrary")),
        name="route",
    )(x, mod, rt)


def _moe_kernel(cnt_ref, h_ref, gates_ref, rank_ref, wa_ref, wb_ref, w2_ref, y_ref, hc_ref, acc_ref,
                *, big, small):
    b, i, e, f = (pl.program_id(k) for k in range(4))
    n_tiles, n_exp, nf = (pl.num_programs(k) for k in (1, 2, 3))
    tm = h_ref.shape[1]
    n = cnt_ref[(b * n_tiles + i) * n_exp + e]
    n_big = n // big
    n_small = (n - n_big * big + small - 1) // small
    align = math.gcd(big, small)
    rank_e = rank_ref[0, 0, pl.ds(e, 1), :]
    gate_e = gates_ref[0, 0, pl.ds(e, 1), :]

    def for_chunks(fn):
        def big_step(c, carry):
            fn(pl.multiple_of(c * big, align), big)
            return carry

        def small_step(c, carry):
            fn(pl.multiple_of(n_big * big + c * small, align), small)
            return carry

        lax.fori_loop(0, n_big, big_step, 0)
        lax.fori_loop(0, n_small, small_step, 0)

    def selected(base, m):
        r = (lax.broadcasted_iota(jnp.int32, (m, tm), 0) + base).astype(F32)
        return rank_e == r

    @pl.when(jnp.logical_and(e == 0, f == 0))
    def _():
        y_ref[...] = jnp.zeros_like(y_ref)

    @pl.when(f == 0)
    def _():
        def gather(base, m):
            p = jnp.where(selected(base, m), 1.0, 0.0).astype(BF16)
            hc_ref[pl.ds(base, m), :] = _dot(p, h_ref[0]).astype(BF16)
            acc_ref[pl.ds(base, m), :] = jnp.zeros((m, acc_ref.shape[1]), F32)
        for_chunks(gather)

    def expert(base, m):
        rows = pl.ds(base, m)
        hc = hc_ref[rows, :]
        a = _dot(hc, wa_ref[0])
        g = _dot(hc, wb_ref[0])
        u = a * _sigmoid(a) * g
        acc_ref[rows, :] += _dot(u.astype(BF16), w2_ref[0])
    for_chunks(expert)

    @pl.when(f == nf - 1)
    def _():
        def scatter(base, m):
            sel = selected(base, m)
            gate_rows = jnp.sum(jnp.where(sel, gate_e, 0.0), axis=1, keepdims=True)
            val = (acc_ref[pl.ds(base, m), :] * gate_rows).astype(BF16)
            y_ref[0] += _dot_tn(jnp.where(sel, 1.0, 0.0).astype(BF16), val)
        for_chunks(scatter)


def _moe(h, gates, rank, counts, w13, w2, tm):
    bsz, seq, d = h.shape
    n_exp, ff = w2.shape[0], w2.shape[1]
    n_rows = gates.shape[2]
    tf = _pick(ff, (512, 256, 128))
    nf = ff // tf
    big = 7 * tm // 32
    small = tm // 16
    per_tile = pl.BlockSpec((1, 1, n_rows, tm), lambda b, i, e, f, cnt: (b, i, 0, 0))
    tok = pl.BlockSpec((1, tm, d), lambda b, i, e, f, cnt: (b, i, 0))
    return pl.pallas_call(
        functools.partial(_moe_kernel, big=big, small=small),
        grid_spec=pltpu.PrefetchScalarGridSpec(
            num_scalar_prefetch=1,
            grid=(bsz, seq // tm, n_exp, nf),
            in_specs=[tok, per_tile, per_tile,
                      pl.BlockSpec((1, d, tf), lambda b, i, e, f, cnt: (e, 0, f)),
                      pl.BlockSpec((1, d, tf), lambda b, i, e, f, cnt: (e, 0, nf + f)),
                      pl.BlockSpec((1, tf, d), lambda b, i, e, f, cnt: (e, f, 0))],
            out_specs=tok,
            scratch_shapes=[pltpu.VMEM((tm, d), BF16), pltpu.VMEM((tm, d), F32)],
        ),
        out_shape=jax.ShapeDtypeStruct((bsz, seq, d), F32),
        compiler_params=_params(("parallel", "parallel", "arbitrary", "arbitrary")),
        name="moe",
    )(counts, h, gates, rank, w13, w13, w2)


def _residual_ln_kernel(x_ref, y_ref, m_ref, g_ref, b_ref, o_ref, *, alpha):
    gate = m_ref[0][GATE2:GATE2 + 1]
    o_ref[0] = _layer_norm(alpha * x_ref[0] + gate * y_ref[0], g_ref[...], b_ref[...])


def _residual_ln(x, y, mod, ln_g, ln_b, alpha):
    bsz, seq, d = x.shape
    tm = _pick(seq, (512, 256, 128))
    tok = pl.BlockSpec((1, tm, d), lambda b, i: (b, i, 0))
    vec = pl.BlockSpec((1, d), lambda b, i: (0, 0))
    return pl.pallas_call(
        functools.partial(_residual_ln_kernel, alpha=alpha),
        grid=(bsz, seq // tm),
        in_specs=[tok, tok, pl.BlockSpec((1, 6, d), lambda b, i: (b, 0, 0)), vec, vec],
        out_specs=tok,
        out_shape=jax.ShapeDtypeStruct((bsz, seq, d), F32),
        compiler_params=_params(("parallel", "parallel")),
        name="residual_ln",
    )(x, y, mod, ln_g.reshape(1, d), ln_b.reshape(1, d))


def _moe_layer(x, mod, router_w, w13, w2, ln_g, ln_b, alpha):
    bsz, seq, d = x.shape
    tm = _pick(seq, (2048, 1024, 512, 256, 128))
    h, gates, rank, cnt = _route(x, mod, router_w, tm)
    counts = cnt[:, :, :w2.shape[0], 0].astype(jnp.int32).reshape(-1)
    y = _moe(h, gates, rank, counts, w13, w2, tm)
    return _residual_ln(x, y, mod, ln_g, ln_b, alpha)


def kernel(x, c, ada_w, ada_b, ln1_g, ln1_b, ln2_g, ln2_b, sb_w_qkv, sb_w_o, hg_w_in, hg_norm_g,
           hg_w_o, hg_lb_logits, ffn_w13, ffn_w2, moe_router, moe_w13, moe_w2):
    bsz, seq, d = x.shape
    depth = ada_w.shape[0]
    alpha = (2.0 * depth) ** 0.25

    lb_table = _lb_table(hg_lb_logits)
    mod_all = _modulation(c, ada_w, ada_b).reshape(depth, bsz, 6, d)

    for layer in range(depth):
        mod = mod_all[layer]
        j = layer // 2
        if layer % 2 == 0:
            qkv = _inproj(x, mod, sb_w_qkv[j].astype(BF16))
            y = _sb_attention(qkv, d)
            x = _outproj_ln(y, sb_w_o[j].astype(BF16), x, mod, ln1_g[layer], ln1_b[layer], alpha)
            x = _ffn(x, mod, ffn_w13[j].astype(BF16), ffn_w2[j].astype(BF16),
                     ln2_g[layer], ln2_b[layer], alpha)
        else:
            qfig = _inproj(x, mod, hg_w_in[j].astype(BF16))
            y = _hgrn(qfig, lb_table[layer], hg_norm_g[j], d)
            x = _outproj_ln(y, hg_w_o[j].astype(BF16), x, mod, ln1_g[layer], ln1_b[layer], alpha)
            x = _moe_layer(x, mod, moe_router[j], moe_w13[j].astype(BF16), moe_w2[j].astype(BF16),
                           ln2_g[layer], ln2_b[layer], alpha)
    return x
```

```python
import functools
import math

import jax
import jax.numpy as jnp
from jax import lax
from jax.experimental import pallas as pl
from jax.experimental.pallas import tpu as pltpu

F32 = jnp.float32
BF16 = jnp.bfloat16

LANES = 128
SB_HEAD_DIM = 64
HG_HEAD_DIM = 128
HG_CHUNK = 32
LN_EPS = 1e-5
RMS_EPS = 1e-6
VMEM_LIMIT_BYTES = 56 * 1024 * 1024
SB_UNDERFLOW_BOUND = -105.0

SHIFT1, SCALE1, GATE1, SHIFT2, SCALE2, GATE2 = range(6)


def _params(sem):
    return pltpu.CompilerParams(dimension_semantics=sem, vmem_limit_bytes=VMEM_LIMIT_BYTES)


def _pick(n, candidates):
    for c in candidates:
        if n % c == 0:
            return c
    raise ValueError(f"no tile in {candidates} divides {n}")


def _sigmoid(x):
    return 1.0 / (1.0 + jnp.exp(-x))


def _dot(a, b):
    return jnp.dot(a, b, preferred_element_type=F32)


def _dot_nt(a, b):
    return lax.dot_general(a, b, (((1,), (1,)), ((), ())), preferred_element_type=F32)


def _dot_tn(a, b):
    return lax.dot_general(a, b, (((0,), (0,)), ((), ())), preferred_element_type=F32)


def _split_dot(x, m):
    hi = x.astype(BF16)
    lo = (x - hi.astype(F32)).astype(BF16)
    return _dot(hi, m) + _dot(lo, m)


def _layer_norm(r, g, b):
    mu = jnp.mean(r, axis=-1, keepdims=True)
    d = r - mu
    var = jnp.mean(d * d, axis=-1, keepdims=True)
    return d * lax.rsqrt(var + LN_EPS) * g + b


def _lb_table_kernel(g_ref, o_ref):
    g = g_ref[...]
    e = jnp.exp(g - jnp.max(g, axis=0, keepdims=True))
    p = e / jnp.sum(e, axis=0, keepdims=True)
    run = jnp.zeros_like(p[0:1])
    for l in range(g.shape[0]):
        run = run + p[l:l + 1]
        o_ref[l:l + 1, :] = run - p[0:1]


def _lb_table(logits):
    return pl.pallas_call(
        _lb_table_kernel,
        out_shape=jax.ShapeDtypeStruct(logits.shape, F32),
        name="lb_table",
    )(logits.astype(F32))


def _modulation_kernel(c_ref, w_ref, b_ref, o_ref):
    c = c_ref[...]
    cond = c * _sigmoid(c)
    y = jnp.dot(cond, w_ref[0], preferred_element_type=F32, precision=lax.Precision.HIGHEST)
    o_ref[0] = y + b_ref[0]


def _modulation(c, ada_w, ada_b):
    depth, d, n = ada_w.shape
    bsz = c.shape[0]
    tn = _pick(n, (1536, 1024, 512, 256, 128))
    return pl.pallas_call(
        _modulation_kernel,
        grid=(depth, n // tn),
        in_specs=[
            pl.BlockSpec((bsz, d), lambda l, j: (0, 0)),
            pl.BlockSpec((1, d, tn), lambda l, j: (l, 0, j)),
            pl.BlockSpec((1, 1, tn), lambda l, j: (l, 0, j)),
        ],
        out_specs=pl.BlockSpec((1, bsz, tn), lambda l, j: (l, 0, j)),
        out_shape=jax.ShapeDtypeStruct((depth, bsz, n), F32),
        compiler_params=_params(("parallel", "parallel")),
        name="modulation",
    )(c, ada_w, ada_b.reshape(depth, 1, n))


def _inproj_kernel(x_ref, m_ref, w_ref, o_ref, h_ref, *, n_chunk):
    m = m_ref[0]
    h_ref[...] = (x_ref[0] * (1.0 + m[SCALE1:SCALE1 + 1]) + m[SHIFT1:SHIFT1 + 1]).astype(BF16)
    for n in range(w_ref.shape[1] // n_chunk):
        cols = slice(n * n_chunk, (n + 1) * n_chunk)
        o_ref[0, :, cols] = _dot(h_ref[...], w_ref[:, cols]).astype(o_ref.dtype)


def _inproj(x, mod, w):
    bsz, seq, d = x.shape
    n = w.shape[1]
    tm = _pick(seq, (512, 256, 128))
    n_chunk = _pick(n, (512, 256, 128))
    return pl.pallas_call(
        functools.partial(_inproj_kernel, n_chunk=n_chunk),
        grid=(bsz, seq // tm),
        in_specs=[
            pl.BlockSpec((1, tm, d), lambda b, i: (b, i, 0)),
            pl.BlockSpec((1, 6, d), lambda b, i: (b, 0, 0)),
            pl.BlockSpec((d, n), lambda b, i: (0, 0)),
        ],
        out_specs=pl.BlockSpec((1, tm, n), lambda b, i: (b, i, 0)),
        out_shape=jax.ShapeDtypeStruct((bsz, seq, n), BF16),
        scratch_shapes=[pltpu.VMEM((tm, d), BF16)],
        compiler_params=_params(("parallel", "parallel")),
        name="inproj",
    )(x, mod, w)


def _sb_attention_kernel(q_ref, k_ref, v_ref, o_ref, qs_ref, carry_ref, acc_ref, *, blk):
    seq, width = q_ref.shape[1], q_ref.shape[2]
    n_pair = width // LANES
    n_head = 2 * n_pair
    scale = SB_HEAD_DIM ** -0.5

    lane = lax.broadcasted_iota(jnp.int32, (blk, LANES), 1)
    head_lanes = [lane < SB_HEAD_DIM, lane >= SB_HEAD_DIM]
    row = lax.broadcasted_iota(jnp.int32, (blk, blk), 0)
    col = lax.broadcasted_iota(jnp.int32, (blk, blk), 1)
    suffix = jnp.concatenate(
        [jnp.where(row > col, 1.0, 0.0), jnp.ones((blk, blk), F32)], axis=1).astype(BF16)
    srow = lax.broadcasted_iota(jnp.int32, (n_head * blk, blk), 0) & (blk - 1)
    scol = lax.broadcasted_iota(jnp.int32, (n_head * blk, blk), 1)
    diag_strict = scol < srow

    def key_block(k0, on_diagonal):
        z = jnp.concatenate(
            [_dot_nt(qs_ref[p * 2 * blk:(p + 1) * 2 * blk, :],
                     k_ref[0, pl.ds(k0, blk), p * LANES:(p + 1) * LANES]) for p in range(n_pair)],
            axis=0) * scale
        log_1m_beta = -(jnp.maximum(z, 0.0) + jnp.log(1.0 + jnp.exp(-jnp.abs(z))))
        if on_diagonal:
            log_1m_beta = jnp.where(diag_strict, log_1m_beta, 0.0)
        log_beta = log_1m_beta + z
        sums = _split_dot(log_1m_beta, suffix)
        c_old = carry_ref[...]
        a = jnp.exp(log_beta + sums[:, :blk] + c_old)
        if on_diagonal:
            a = jnp.where(diag_strict, a, 0.0)
        a = a.astype(BF16)
        carry_ref[...] = c_old + sums[:, blk:]
        for p in range(n_pair):
            v2 = v_ref[0, pl.ds(k0, blk), p * LANES:(p + 1) * LANES]
            upd = None
            for hh in range(2):
                h = 2 * p + hh
                vm = jnp.where(head_lanes[hh], v2, jnp.zeros_like(v2))
                t = _dot(a[h * blk:(h + 1) * blk], vm)
                upd = t if upd is None else upd + t
            acc_ref[p] += upd

    def all_underflowed():
        return (jnp.max(carry_ref[...]) < SB_UNDERFLOW_BOUND).astype(jnp.int32)

    n_peeled = 2

    def query_block(qi, _):
        q0 = pl.multiple_of(qi * blk, blk)
        carry_ref[...] = jnp.zeros_like(carry_ref)
        acc_ref[...] = jnp.zeros_like(acc_ref)
        for p in range(n_pair):
            q2 = q_ref[0, pl.ds(q0, blk), p * LANES:(p + 1) * LANES]
            for hh in range(2):
                h = 2 * p + hh
                qs_ref[h * blk:(h + 1) * blk, :] = jnp.where(head_lanes[hh], q2, jnp.zeros_like(q2))

        key_block(q0, True)
        for n in range(1, n_peeled + 1):
            @pl.when(qi >= n)
            def _():
                key_block(pl.multiple_of((qi - n) * blk, blk), False)

        def not_done(state):
            j, done = state
            return jnp.logical_and(j >= 0, done == 0)

        def sweep(state):
            j, _ = state
            key_block(pl.multiple_of(j * blk, blk), False)
            return j - 1, all_underflowed()

        lax.while_loop(not_done, sweep, (qi - (n_peeled + 1), all_underflowed()))
        for p in range(n_pair):
            o_ref[0, pl.ds(q0, blk), p * LANES:(p + 1) * LANES] = acc_ref[p].astype(o_ref.dtype)
        return 0

    lax.fori_loop(0, seq // blk, query_block, 0)


def _sb_attention(qkv, d):
    bsz, seq, _ = qkv.shape
    width = _pick(d, (256, 128))
    n_grp = d // width
    blk = 128
    spec = lambda off: pl.BlockSpec((1, seq, width), lambda b, g: (b, 0, off + g))
    return pl.pallas_call(
        functools.partial(_sb_attention_kernel, blk=blk),
        grid=(bsz, n_grp),
        in_specs=[spec(0), spec(n_grp), spec(2 * n_grp)],
        out_specs=pl.BlockSpec((1, seq, width), lambda b, g: (b, 0, g)),
        out_shape=jax.ShapeDtypeStruct((bsz, seq, d), BF16),
        scratch_shapes=[
            pltpu.VMEM((width // SB_HEAD_DIM * blk, LANES), BF16),
            pltpu.VMEM((width // SB_HEAD_DIM * blk, blk), F32),
            pltpu.VMEM((width // LANES, blk, LANES), F32),
        ],
        compiler_params=_params(("parallel", "parallel")),
        name="sb_attention",
    )(qkv, qkv, qkv)


def _hgrn_kernel(q_ref, f_ref, i_ref, g_ref, lb_ref, ng_ref, o_ref, state_ref):
    ts, d = q_ref.shape[1], q_ref.shape[2]
    n_heads = d // HG_HEAD_DIM
    c = HG_CHUNK
    mid = c // 2

    @pl.when(pl.program_id(1) == 0)
    def _():
        state_ref[...] = jnp.zeros_like(state_ref)

    row = lax.broadcasted_iota(jnp.int32, (c, c), 0)
    col = lax.broadcasted_iota(jnp.int32, (c, c), 1)
    causal = col <= row
    prefix = jnp.where(causal, 1.0, 0.0).astype(BF16)
    lb = lb_ref[...]
    one_m_lb = 1.0 - lb
    norm_g = jnp.concatenate([ng_ref[...]] * n_heads, axis=1)

    def chunk(ci, states):
        rows = pl.ds(pl.multiple_of(ci * c, c), c)
        qc = q_ref[0, rows, :].astype(F32)
        fc = f_ref[0, rows, :].astype(F32)
        vc = i_ref[0, rows, :]
        gc = g_ref[0, rows, :].astype(F32)
        e = jnp.exp(-jnp.abs(fc))
        big = 1.0 / (1.0 + e)
        small = e * big
        sig_pos = jnp.where(fc >= 0, big, small)
        sig_neg = jnp.where(fc >= 0, small, big)
        log_forget = jnp.log(lb + one_m_lb * sig_pos)
        kc = one_m_lb * sig_neg
        hi = log_forget.astype(BF16)
        lo = (log_forget - hi.astype(F32)).astype(BF16)
        cum = _dot(prefix, hi) + _dot(prefix, lo)
        g_mid = cum[mid - 1:mid]
        g_end = cum[c - 1:c]
        qd = (qc * jnp.exp(cum - g_mid)).astype(BF16)
        kd = (kc * jnp.exp(g_mid - cum)).astype(BF16)
        q_in = (qc * jnp.exp(cum)).astype(BF16)
        k_end = (kc * jnp.exp(g_end - cum)).astype(BF16)
        decay = jnp.exp(g_end)
        outs, new_states = [], []
        for h in range(n_heads):
            lanes = slice(h * HG_HEAD_DIM, (h + 1) * HG_HEAD_DIM)
            st = states[h]
            scores = jnp.where(causal, _dot_nt(qd[:, lanes], kd[:, lanes]), 0.0)
            o = _dot(scores.astype(BF16), vc[:, lanes]) + _dot_nt(q_in[:, lanes], st.astype(BF16))
            new_states.append(st * decay[:, lanes] + _dot_tn(vc[:, lanes], k_end[:, lanes]))
            ms = jnp.mean(o * o, axis=-1, keepdims=True)
            outs.append(o * lax.rsqrt(ms + RMS_EPS))
        y = jnp.concatenate(outs, axis=1) * norm_g * (gc * _sigmoid(gc))
        o_ref[0, rows, :] = y.astype(o_ref.dtype)
        return tuple(new_states)

    states = lax.fori_loop(0, ts // c, chunk, tuple(state_ref[h] for h in range(n_heads)), unroll=2)
    for h in range(n_heads):
        state_ref[h] = states[h]


def _hgrn(qfig, lb_row, norm_g, d):
    bsz, seq, _ = qfig.shape
    ts = _pick(seq, (512, 256, 128))
    n_heads = d // HG_HEAD_DIM
    spec = lambda off: pl.BlockSpec((1, ts, d), lambda b, s: (b, s, off))
    return pl.pallas_call(
        _hgrn_kernel,
        grid=(bsz, seq // ts),
        in_specs=[spec(0), spec(1), spec(2), spec(3),
                  pl.BlockSpec((1, d), lambda b, s: (0, 0)),
                  pl.BlockSpec((1, HG_HEAD_DIM), lambda b, s: (0, 0))],
        out_specs=pl.BlockSpec((1, ts, d), lambda b, s: (b, s, 0)),
        out_shape=jax.ShapeDtypeStruct((bsz, seq, d), BF16),
        scratch_shapes=[pltpu.VMEM((n_heads, HG_HEAD_DIM, HG_HEAD_DIM), F32)],
        compiler_params=_params(("parallel", "arbitrary")),
        name="hgrn",
    )(qfig, qfig, qfig, qfig, lb_row.reshape(1, d), norm_g.reshape(1, HG_HEAD_DIM).astype(F32))


def _outproj_ln_kernel(y_ref, w_ref, x_ref, m_ref, g_ref, b_ref, o_ref, *, alpha):
    y = _dot(y_ref[0], w_ref[...])
    gate = m_ref[0][GATE1:GATE1 + 1]
    o_ref[0] = _layer_norm(alpha * x_ref[0] + gate * y, g_ref[...], b_ref[...])


def _outproj_ln(y, w, x, mod, ln_g, ln_b, alpha):
    bsz, seq, d = x.shape
    tm = _pick(seq, (512, 256, 128))
    tok = pl.BlockSpec((1, tm, d), lambda b, i: (b, i, 0))
    vec = pl.BlockSpec((1, d), lambda b, i: (0, 0))
    return pl.pallas_call(
        functools.partial(_outproj_ln_kernel, alpha=alpha),
        grid=(bsz, seq // tm),
        in_specs=[tok, pl.BlockSpec((d, d), lambda b, i: (0, 0)), tok,
                  pl.BlockSpec((1, 6, d), lambda b, i: (b, 0, 0)), vec, vec],
        out_specs=tok,
        out_shape=jax.ShapeDtypeStruct((bsz, seq, d), F32),
        compiler_params=_params(("parallel", "parallel")),
        name="outproj_ln",
    )(y, w, x, mod, ln_g.reshape(1, d), ln_b.reshape(1, d))


def _ffn_kernel(x_ref, m_ref, wa_ref, wb_ref, w2_ref, g_ref, b_ref, o_ref, h_ref, acc_ref, *, alpha):
    f = pl.program_id(2)
    m = m_ref[0]

    @pl.when(f == 0)
    def _():
        h_ref[...] = (x_ref[0] * (1.0 + m[SCALE2:SCALE2 + 1]) + m[SHIFT2:SHIFT2 + 1]).astype(BF16)
        acc_ref[...] = jnp.zeros_like(acc_ref)

    h = h_ref[...]
    a = _dot(h, wa_ref[...])
    b = _dot(h, wb_ref[...])
    u = a * _sigmoid(a) * b
    acc_ref[...] += _dot(u.astype(BF16), w2_ref[...])

    @pl.when(f == pl.num_programs(2) - 1)
    def _():
        r = alpha * x_ref[0] + m[GATE2:GATE2 + 1] * acc_ref[...]
        o_ref[0] = _layer_norm(r, g_ref[...], b_ref[...])


def _ffn(x, mod, w13, w2, ln_g, ln_b, alpha):
    bsz, seq, d = x.shape
    ff = w2.shape[0]
    tm = _pick(seq, (1024, 512, 256, 128))
    tf = _pick(ff, (512, 256, 128))
    nf = ff // tf
    tok = pl.BlockSpec((1, tm, d), lambda b, i, f: (b, i, 0))
    vec = pl.BlockSpec((1, d), lambda b, i, f: (0, 0))
    return pl.pallas_call(
        functools.partial(_ffn_kernel, alpha=alpha),
        grid=(bsz, seq // tm, nf),
        in_specs=[tok, pl.BlockSpec((1, 6, d), lambda b, i, f: (b, 0, 0)),
                  pl.BlockSpec((d, tf), lambda b, i, f: (0, f)),
                  pl.BlockSpec((d, tf), lambda b, i, f: (0, nf + f)),
                  pl.BlockSpec((tf, d), lambda b, i, f: (f, 0)),
                  vec, vec],
        out_specs=tok,
        out_shape=jax.ShapeDtypeStruct((bsz, seq, d), F32),
        scratch_shapes=[pltpu.VMEM((tm, d), BF16), pltpu.VMEM((tm, d), F32)],
        compiler_params=_params(("parallel", "parallel", "arbitrary")),
        name="ffn",
    )(x, mod, w13, w13, w2, ln_g.reshape(1, d), ln_b.reshape(1, d))


def _route_kernel(x_ref, m_ref, rt_ref, h_ref, gates_ref, rank_ref, cnt_ref, before_ref, *, n_experts):
    tm = x_ref.shape[1]
    n_rows = rt_ref.shape[0]

    @pl.when(jnp.logical_and(pl.program_id(0) == 0, pl.program_id(1) == 0))
    def _():
        step = 256 if tm % 256 == 0 else tm
        for r in range(0, tm, step):
            s = lax.broadcasted_iota(jnp.int32, (step, tm), 0) + r
            t = lax.broadcasted_iota(jnp.int32, (step, tm), 1)
            before_ref[r:r + step, :] = jnp.where(s < t, 1.0, 0.0).astype(BF16)

    m = m_ref[0]
    h = (x_ref[0] * (1.0 + m[SCALE2:SCALE2 + 1]) + m[SHIFT2:SHIFT2 + 1]).astype(BF16)
    h_ref[0] = h
    logits = _dot_nt(rt_ref[...], h)
    sub = lax.broadcasted_iota(jnp.int32, logits.shape, 0)
    logits = jnp.where(sub < n_experts, logits, -jnp.inf)
    m1 = jnp.max(logits, axis=0, keepdims=True)
    i1 = jnp.min(jnp.where(logits == m1, sub, n_rows), axis=0, keepdims=True)
    rest = jnp.where(sub == i1, -jnp.inf, logits)
    m2 = jnp.max(rest, axis=0, keepdims=True)
    i2 = jnp.min(jnp.where(rest == m2, sub, n_rows), axis=0, keepdims=True)
    t = jnp.exp(m2 - m1)
    w1 = 1.0 / (1.0 + t)
    first, second = sub == i1, sub == i2
    routed = jnp.logical_or(first, second)
    ind = jnp.where(routed, 1.0, 0.0).astype(F32)
    rank = _dot(ind.astype(BF16), before_ref[...])
    gates_ref[0, 0] = jnp.where(first, w1, 0.0) + jnp.where(second, t * w1, 0.0)
    rank_ref[0, 0] = jnp.where(routed, rank, -1.0)
    cnt_ref[0, 0] = jnp.broadcast_to(jnp.sum(ind, axis=1, keepdims=True), (n_rows, LANES))


def _route(x, mod, router_w, tm):
    bsz, seq, d = x.shape
    n_experts = router_w.shape[1]
    n_rows = -(-n_experts // 8) * 8
    nt = seq // tm
    rt = jnp.zeros((n_rows, d), BF16).at[:n_experts].set(router_w.T.astype(BF16))
    per_tile = lambda w: pl.BlockSpec((1, 1, n_rows, w), lambda b, i: (b, i, 0, 0))
    return pl.pallas_call(
        functools.partial(_route_kernel, n_experts=n_experts),
        grid=(bsz, nt),
        in_specs=[pl.BlockSpec((1, tm, d), lambda b, i: (b, i, 0)),
                  pl.BlockSpec((1, 6, d), lambda b, i: (b, 0, 0)),
                  pl.BlockSpec((n_rows, d), lambda b, i: (0, 0))],
        out_specs=[pl.BlockSpec((1, tm, d), lambda b, i: (b, i, 0)),
                   per_tile(tm), per_tile(tm), per_tile(LANES)],
        out_shape=[jax.ShapeDtypeStruct((bsz, seq, d), BF16),
                   jax.ShapeDtypeStruct((bsz, nt, n_rows, tm), F32),
                   jax.ShapeDtypeStruct((bsz, nt, n_rows, tm), F32),
                   jax.ShapeDtypeStruct((bsz, nt, n_rows, LANES), F32)],
        scratch_shapes=[pltpu.VMEM((tm, tm), BF16)],
        compiler_params=_params(("arbitrary", "arbitrary")),
        name="route",
    )(x, mod, rt)


def _moe_kernel(cnt_ref, h_ref, gates_ref, rank_ref, wa_ref, wb_ref, w2_ref, y_ref, hc_ref, acc_ref,
                *, big, small):
    b, i, e, f = (pl.program_id(k) for k in range(4))
    n_tiles, n_exp, nf = (pl.num_programs(k) for k in (1, 2, 3))
    tm = h_ref.shape[1]
    n = cnt_ref[(b * n_tiles + i) * n_exp + e]
    n_big = n // big
    n_small = (n - n_big * big + small - 1) // small
    align = math.gcd(big, small)
    rank_e = rank_ref[0, 0, pl.ds(e, 1), :]
    gate_e = gates_ref[0, 0, pl.ds(e, 1), :]

    def for_chunks(fn):
        def big_step(c, carry):
            fn(pl.multiple_of(c * big, align), big)
            return carry

        def small_step(c, carry):
            fn(pl.multiple_of(n_big * big + c * small, align), small)
            return carry

        lax.fori_loop(0, n_big, big_step, 0)
        lax.fori_loop(0, n_small, small_step, 0)

    def selected(base, m):
        r = (lax.broadcasted_iota(jnp.int32, (m, tm), 0) + base).astype(F32)
        return rank_e == r

    @pl.when(jnp.logical_and(e == 0, f == 0))
    def _():
        y_ref[...] = jnp.zeros_like(y_ref)

    @pl.when(f == 0)
    def _():
        def gather(base, m):
            p = jnp.where(selected(base, m), 1.0, 0.0).astype(BF16)
            hc_ref[pl.ds(base, m), :] = _dot(p, h_ref[0]).astype(BF16)
            acc_ref[pl.ds(base, m), :] = jnp.zeros((m, acc_ref.shape[1]), F32)
        for_chunks(gather)

    def expert(base, m):
        rows = pl.ds(base, m)
        hc = hc_ref[rows, :]
        a = _dot(hc, wa_ref[0])
        g = _dot(hc, wb_ref[0])
        u = a * _sigmoid(a) * g
        acc_ref[rows, :] += _dot(u.astype(BF16), w2_ref[0])
    for_chunks(expert)

    @pl.when(f == nf - 1)
    def _():
        def scatter(base, m):
            sel = selected(base, m)
            gate_rows = jnp.sum(jnp.where(sel, gate_e, 0.0), axis=1, keepdims=True)
            val = (acc_ref[pl.ds(base, m), :] * gate_rows).astype(BF16)
            y_ref[0] += _dot_tn(jnp.where(sel, 1.0, 0.0).astype(BF16), val)
        for_chunks(scatter)


def _moe(h, gates, rank, counts, w13, w2, tm):
    bsz, seq, d = h.shape
    n_exp, ff = w2.shape[0], w2.shape[1]
    n_rows = gates.shape[2]
    tf = _pick(ff, (896, 512, 256, 128))
    nf = ff // tf
    big = 7 * tm // 32
    small = tm // 16
    per_tile = pl.BlockSpec((1, 1, n_rows, tm), lambda b, i, e, f, cnt: (b, i, 0, 0))
    tok = pl.BlockSpec((1, tm, d), lambda b, i, e, f, cnt: (b, i, 0))
    return pl.pallas_call(
        functools.partial(_moe_kernel, big=big, small=small),
        grid_spec=pltpu.PrefetchScalarGridSpec(
            num_scalar_prefetch=1,
            grid=(bsz, seq // tm, n_exp, nf),
            in_specs=[tok, per_tile, per_tile,
                      pl.BlockSpec((1, d, tf), lambda b, i, e, f, cnt: (e, 0, f)),
                      pl.BlockSpec((1, d, tf), lambda b, i, e, f, cnt: (e, 0, nf + f)),
                      pl.BlockSpec((1, tf, d), lambda b, i, e, f, cnt: (e, f, 0))],
            out_specs=tok,
            scratch_shapes=[pltpu.VMEM((tm, d), BF16), pltpu.VMEM((tm, d), F32)],
        ),
        out_shape=jax.ShapeDtypeStruct((bsz, seq, d), F32),
        compiler_params=_params(("parallel", "parallel", "arbitrary", "arbitrary")),
        name="moe",
    )(counts, h, gates, rank, w13, w13, w2)


def _residual_ln_kernel(x_ref, y_ref, m_ref, g_ref, b_ref, o_ref, *, alpha):
    gate = m_ref[0][GATE2:GATE2 + 1]
    o_ref[0] = _layer_norm(alpha * x_ref[0] + gate * y_ref[0], g_ref[...], b_ref[...])


def _residual_ln(x, y, mod, ln_g, ln_b, alpha):
    bsz, seq, d = x.shape
    tm = _pick(seq, (512, 256, 128))
    tok = pl.BlockSpec((1, tm, d), lambda b, i: (b, i, 0))
    vec = pl.BlockSpec((1, d), lambda b, i: (0, 0))
    return pl.pallas_call(
        functools.partial(_residual_ln_kernel, alpha=alpha),
        grid=(bsz, seq // tm),
        in_specs=[tok, tok, pl.BlockSpec((1, 6, d), lambda b, i: (b, 0, 0)), vec, vec],
        out_specs=tok,
        out_shape=jax.ShapeDtypeStruct((bsz, seq, d), F32),
        compiler_params=_params(("parallel", "parallel")),
        name="residual_ln",
    )(x, y, mod, ln_g.reshape(1, d), ln_b.reshape(1, d))


def _moe_layer(x, mod, router_w, w13, w2, ln_g, ln_b, alpha):
    bsz, seq, d = x.shape
    tm = _pick(seq, (2048, 1024, 512, 256, 128))
    h, gates, rank, cnt = _route(x, mod, router_w, tm)
    counts = cnt[:, :, :w2.shape[0], 0].astype(jnp.int32).reshape(-1)
    y = _moe(h, gates, rank, counts, w13, w2, tm)
    return _residual_ln(x, y, mod, ln_g, ln_b, alpha)


def kernel(x, c, ada_w, ada_b, ln1_g, ln1_b, ln2_g, ln2_b, sb_w_qkv, sb_w_o, hg_w_in, hg_norm_g,
           hg_w_o, hg_lb_logits, ffn_w13, ffn_w2, moe_router, moe_w13, moe_w2):
    bsz, seq, d = x.shape
    depth = ada_w.shape[0]
    alpha = (2.0 * depth) ** 0.25

    lb_table = _lb_table(hg_lb_logits)
    mod_all = _modulation(c, ada_w, ada_b).reshape(depth, bsz, 6, d)

    for layer in range(depth):
        mod = mod_all[layer]
        j = layer // 2
        if layer % 2 == 0:
            qkv = _inproj(x, mod, sb_w_qkv[j].astype(BF16))
            y = _sb_attention(qkv, d)
            x = _outproj_ln(y, sb_w_o[j].astype(BF16), x, mod, ln1_g[layer], ln1_b[layer], alpha)
            x = _ffn(x, mod, ffn_w13[j].astype(BF16), ffn_w2[j].astype(BF16),
                     ln2_g[layer], ln2_b[layer], alpha)
        else:
            qfig = _inproj(x, mod, hg_w_in[j].astype(BF16))
            y = _hgrn(qfig, lb_table[layer], hg_norm_g[j], d)
            x = _outproj_ln(y, hg_w_o[j].astype(BF16), x, mod, ln1_g[layer], ln1_b[layer], alpha)
            x = _moe_layer(x, mod, moe_router[j], moe_w13[j].astype(BF16), moe_w2[j].astype(BF16),
                           ln2_g[layer], ln2_b[layer], alpha)
    return x
```
